```python
import jax, jax.numpy as jnp
from jax import lax
import numpy as np

D_MODEL = 4096
BATCH = 2
SEQ = 8192
DEPTH = 1

GRID_W = 64
PLE_DIM = 256
HEAD_DIM = 128
N_Q_HEADS = 16
N_KV_HEADS = 4
Q_PER_KV = N_Q_HEADS // N_KV_HEADS
ATTN_DIM = N_Q_HEADS * HEAD_DIM
KV_DIM = N_KV_HEADS * HEAD_DIM
Q_BLOCK = 128
ROPE_THETA = 10000.0
HG_HEADS = 16
HG_KEY_DIM = 128
HG_VAL_DIM = 128
HG_QK_DIM = HG_HEADS * HG_KEY_DIM
HG_V_DIM = HG_HEADS * HG_VAL_DIM
HG_CHUNK = 64
IN_SIZES = (ATTN_DIM, KV_DIM, KV_DIM, HG_QK_DIM, HG_QK_DIM, HG_QK_DIM, HG_V_DIM, HG_V_DIM)
IN_COLS = sum(IN_SIZES)
D_FF = 11008
CONV_WIDTH = 3
RMS_EPS = 1e-6
LN_EPS = 1e-5
DN_ALPHA = (2.0 * DEPTH) ** 0.25
DN_BETA = (8.0 * DEPTH) ** -0.25

kernel_name = "hybrid_gqa_hgrn2_deepnorm_encoder"


def rms_norm(x, gain):
    xf = x.astype(jnp.float32)
    y = xf * lax.rsqrt(jnp.mean(xf * xf, axis=-1, keepdims=True) + RMS_EPS)
    return y * gain.astype(jnp.float32)


def layer_norm(x, g, b):
    xf = x.astype(jnp.float32)
    mu = jnp.mean(xf, axis=-1, keepdims=True)
    var = jnp.mean(jnp.square(xf - mu), axis=-1, keepdims=True)
    y = (xf - mu) * lax.rsqrt(var + LN_EPS) * g.astype(jnp.float32) + b.astype(jnp.float32)
    return y.astype(x.dtype)


def axial_rope_tables(seq_len):
    rows = seq_len // GRID_W
    row = jnp.repeat(jnp.arange(rows), GRID_W).astype(jnp.float32)
    col = jnp.tile(jnp.arange(GRID_W), rows).astype(jnp.float32)
    sec = HEAD_DIM // 2
    inv = ROPE_THETA ** (-jnp.arange(0, sec, 2, dtype=jnp.float32) / sec)
    ang_r = row[:, None] * inv[None, :]
    ang_c = col[:, None] * inv[None, :]
    ang = jnp.concatenate([ang_r, ang_r, ang_c, ang_c], axis=-1)
    return jnp.cos(ang), jnp.sin(ang)


def apply_axial_rope(x, cos, sin):
    xs = x.reshape(x.shape[:-1] + (2, 2, HEAD_DIM // 4))
    rot = jnp.stack([-xs[..., 1, :], xs[..., 0, :]], axis=-2).reshape(x.shape)
    return x * cos[None, :, None, :] + rot * sin[None, :, None, :]


def gqa_axial_attention(q, k, v, cos, sin, q_gain, k_gain):
    B, S, _ = q.shape
    q = apply_axial_rope(rms_norm(q.reshape(B, S, N_Q_HEADS, HEAD_DIM), q_gain), cos, sin).astype(v.dtype)
    k = apply_axial_rope(rms_norm(k.reshape(B, S, N_KV_HEADS, HEAD_DIM), k_gain), cos, sin).astype(v.dtype)
    v = v.reshape(B, S, N_KV_HEADS, HEAD_DIM)
    n_blk = S // Q_BLOCK
    qb = q.reshape(B, n_blk, Q_BLOCK, N_KV_HEADS, Q_PER_KV, HEAD_DIM).transpose(1, 0, 3, 4, 2, 5)
    scale = HEAD_DIM ** -0.5

    def block(q_blk):
        s = jnp.einsum('bhgqd,bkhd->bhgqk', q_blk, k, preferred_element_type=jnp.float32) * scale
        pr = jax.nn.softmax(s, axis=-1).astype(v.dtype)
        return jnp.einsum('bhgqk,bkhd->bqhgd', pr, v)

    o = lax.map(block, qb)
    return o.transpose(1, 0, 2, 3, 4, 5).reshape(B, S, ATTN_DIM)


def hgrn2_chunk_scan(q, k, v, logf):
    B, S, H, dk = q.shape
    dv = v.shape[-1]
    C = HG_CHUNK
    N = S // C
    q, k, v, g = (t.reshape(B, N, C, H, t.shape[-1]) for t in (q, k, v, logf))
    b = jnp.cumsum(g, axis=2)
    b_ref = b[:, :, C // 2 - 1:C // 2]
    q_in = q * jnp.exp(b - b_ref)
    k_in = k * jnp.exp(b_ref - b)
    A = jnp.einsum('bnthd,bnshd->bnhts', q_in, k_in)
    A = jnp.where(jnp.tril(jnp.ones((C, C), dtype=bool)), A, 0.0)
    o_intra = jnp.einsum('bnhts,bnshe->bnthe', A, v)
    b_last = b[:, :, -1:]
    U = jnp.einsum('bnshd,bnshe->bnhde', k * jnp.exp(b_last - b), v)
    decay = jnp.exp(b[:, :, -1])

    def step(state, inp):
        dec, u = inp
        return dec[..., None] * state + u, state

    _, s_before = lax.scan(step, jnp.zeros((B, H, dk, dv), jnp.float32),
                           (decay.transpose(1, 0, 2, 3), U.transpose(1, 0, 2, 3, 4)))
    s_before = s_before.transpose(1, 0, 2, 3, 4)
    o_inter = jnp.einsum('bnthd,bnhde->bnthe', q * jnp.exp(b), s_before)
    return (o_intra + o_inter).reshape(B, S, H, dv)


def hgrn2_bidirectional(q_raw, zf_raw, zb_raw, i_raw, g_raw, lb_fwd, lb_bwd, norm_gain):
    B, S, _ = q_raw.shape
    heads = lambda t, d: t.astype(jnp.float32).reshape(B, S, HG_HEADS, d)
    q = jax.nn.silu(heads(q_raw, HG_KEY_DIM))
    v = heads(i_raw, HG_VAL_DIM)

    def gates(z_raw, lb):
        z = heads(z_raw, HG_KEY_DIM)
        lb = lb.reshape(HG_HEADS, HG_KEY_DIM)
        logf = jnp.logaddexp(jnp.log(lb), jnp.log1p(-lb) + jax.nn.log_sigmoid(z))
        return (1.0 - lb) * jax.nn.sigmoid(-z), logf

    k_f, g_f = gates(zf_raw, lb_fwd)
    k_b, g_b = gates(zb_raw, lb_bwd)
    flip = lambda t: jnp.flip(t, axis=1)
    o = hgrn2_chunk_scan(q, k_f, v, g_f) + flip(hgrn2_chunk_scan(flip(q), flip(k_b), flip(v), flip(g_b)))
    o = rms_norm(o, norm_gain.reshape(HG_HEADS, HG_VAL_DIM)).reshape(B, S, HG_V_DIM)
    return (o * jax.nn.silu(g_raw.astype(jnp.float32))).astype(q_raw.dtype)


def setup_inputs(seed: int = 0) -> dict:
    key = jax.random.key(seed)
    ks = jax.random.split(key, 24)
    nrm = lambda k, shape, scale: jax.random.normal(k, shape, jnp.float32) * scale
    gain = lambda k, shape: 1.0 + 0.01 * jax.random.normal(k, shape, jnp.float32)
    L, D = DEPTH, D_MODEL
    return {
        "x": nrm(ks[0], (BATCH, SEQ, D), 1.0),
        "p": nrm(ks[1], (L, BATCH, SEQ, PLE_DIM), 1.0),
        "w_in": nrm(ks[2], (L, D, IN_COLS), D ** -0.5),
        "q_norm": gain(ks[3], (L, HEAD_DIM)),
        "k_norm": gain(ks[4], (L, HEAD_DIM)),
        "lb_logits": nrm(ks[5], (2, L + 1, HG_QK_DIM), 0.5),
        "hg_norm": gain(ks[6], (L, HG_V_DIM)),
        "w_pa": nrm(ks[7], (L, ATTN_DIM, D), ATTN_DIM ** -0.5 * DN_BETA),
        "w_pb": nrm(ks[8], (L, HG_V_DIM, D), HG_V_DIM ** -0.5 * DN_BETA),
        "w_gate": nrm(ks[9], (L, D, 2 * D), D ** -0.5),
        "b_gate": nrm(ks[10], (L, 2 * D), 0.01),
        "w_o": nrm(ks[11], (L, D, D), D ** -0.5 * DN_BETA),
        "ln1_g": gain(ks[12], (L, D)),
        "ln1_b": nrm(ks[13], (L, D), 0.01),
        "w_up": nrm(ks[14], (L, D, 2 * D_FF), D ** -0.5),
        "conv_w": nrm(ks[15], (L, CONV_WIDTH, D_FF), CONV_WIDTH ** -0.5),
        "conv_b": nrm(ks[16], (L, D_FF), 0.01),
        "w_down": nrm(ks[17], (L, D_FF, D), D_FF ** -0.5 * DN_BETA),
        "ln2_g": gain(ks[18], (L, D)),
        "ln2_b": nrm(ks[19], (L, D), 0.01),
        "w_pg": nrm(ks[20], (L, D, D), D ** -0.5),
        "w_ple": nrm(ks[21], (L, PLE_DIM, D), PLE_DIM ** -0.5),
    }


def reference(x, p, w_in, q_norm, k_norm, lb_logits, hg_norm, w_pa, w_pb, w_gate, b_gate, w_o,
              ln1_g, ln1_b, w_up, conv_w, conv_b, w_down, ln2_g, ln2_b, w_pg, w_ple):
    B, S, _ = x.shape
    cos, sin = axial_rope_tables(S)
    lb_all = jnp.cumsum(jax.nn.softmax(lb_logits.astype(jnp.float32), axis=1), axis=1)
    split_at = np.cumsum(IN_SIZES)[:-1].tolist()
    for i in range(DEPTH):
        proj = x @ w_in[i]
        qa, ka, va, qh, zf, zb, ih, gh = jnp.split(proj, split_at, axis=-1)
        y_attn = gqa_axial_attention(qa, ka, va, cos, sin, q_norm[i], k_norm[i])
        y_hgrn = hgrn2_bidirectional(qh, zf, zb, ih, gh, lb_all[0, i], lb_all[1, i], hg_norm[i])
        g_attn, g_hgrn = jnp.split(jax.nn.sigmoid(x @ w_gate[i] + b_gate[i]), 2, axis=-1)
        mixed = (g_attn * (y_attn @ w_pa[i]) + g_hgrn * (y_hgrn @ w_pb[i])) @ w_o[i]
        x = layer_norm(DN_ALPHA * x + mixed, ln1_g[i], ln1_b[i])
        u, gpre = jnp.split(x @ w_up[i], 2, axis=-1)
        gp = jnp.pad(gpre, ((0, 0), (1, 1), (0, 0)))
        gconv = (gp[:, :-2] * conv_w[i, 0] + gp[:, 1:-1] * conv_w[i, 1]
                 + gp[:, 2:] * conv_w[i, 2] + conv_b[i])
        ffn = (jax.nn.silu(gconv) * u) @ w_down[i]
        x = layer_norm(DN_ALPHA * x + ffn, ln2_g[i], ln2_b[i])
        x = x + jax.nn.sigmoid(x @ w_pg[i]) * (p[i] @ w_ple[i])
    return x
```

```python
import functools

import jax
import jax.numpy as jnp
from jax import lax
from jax.experimental import pallas as pl
from jax.experimental.pallas import tpu as pltpu

F32 = jnp.float32
BF16 = jnp.bfloat16

GRID_W = 64
HEAD_DIM = 128
N_Q_HEADS = 16
N_KV_HEADS = 4
Q_PER_KV = N_Q_HEADS // N_KV_HEADS
ROPE_THETA = 10000.0
HG_HEADS = 16
HG_DIM = 128
HG_CHUNK = 64
RMS_EPS = 1e-6
LN_EPS = 1e-5

LANES = 128
TRI_BLOCK = 256
VMEM_LIMIT = 56 * 1024 * 1024

NT_DIMS = (((1,), (1,)), ((), ()))
TN_DIMS = (((0,), (0,)), ((), ()))


def _params(*sem):
    return pltpu.CompilerParams(dimension_semantics=sem, vmem_limit_bytes=VMEM_LIMIT)


def _dot(a, b):
    return jnp.dot(a, b, preferred_element_type=F32)


def _sigmoid(v):
    return 1.0 / (1.0 + jnp.exp(-v))


def _qkv_kernel(x_ref, w_ref, cos_ref, sa_ref, sb_ref, qg_ref, kg_ref, o_ref, *, nqb, nkb, q_scale):
    j = pl.program_id(1)
    acc = _dot(x_ref[...], w_ref[...])

    def norm_rope(gain_ref, scale):
        cos, sa, sb = cos_ref[...], sa_ref[...], sb_ref[...]
        gain = gain_ref[...]
        for h in range(acc.shape[1] // HEAD_DIM):
            sl = slice(h * HEAD_DIM, (h + 1) * HEAD_DIM)
            blk = acc[:, sl]
            ms = jnp.mean(blk * blk, axis=-1, keepdims=True)
            y = blk * lax.rsqrt(ms + RMS_EPS) * gain
            rot = (pltpu.roll(y, HEAD_DIM - HEAD_DIM // 4, 1) * sa
                   + pltpu.roll(y, HEAD_DIM // 4, 1) * sb)
            o_ref[:, sl] = ((y * cos + rot) * scale).astype(o_ref.dtype)

    @pl.when(j < nqb)
    def _():
        norm_rope(qg_ref, q_scale)

    @pl.when((j >= nqb) & (j < nqb + nkb))
    def _():
        norm_rope(kg_ref, 1.0)

    @pl.when(j >= nqb + nkb)
    def _():
        o_ref[...] = acc.astype(o_ref.dtype)


def _qkv_proj(xb, w_in_b, cos, sa, sb, q_gain, k_gain, seq, *, tm, tn):
    m, d = xb.shape
    attn_dim = N_Q_HEADS * HEAD_DIM
    kv_dim = N_KV_HEADS * HEAD_DIM
    n = attn_dim + 2 * kv_dim
    nsb = seq // tm
    tab = pl.BlockSpec((tm, HEAD_DIM), lambda i, j: (i % nsb, 0))
    vec = pl.BlockSpec((1, HEAD_DIM), lambda i, j: (0, 0))
    return pl.pallas_call(
        functools.partial(_qkv_kernel, nqb=attn_dim // tn, nkb=kv_dim // tn, q_scale=HEAD_DIM ** -0.5),
        grid=(m // tm, n // tn),
        in_specs=[pl.BlockSpec((tm, d), lambda i, j: (i, 0)),
                  pl.BlockSpec((d, tn), lambda i, j: (0, j)),
                  tab, tab, tab, vec, vec],
        out_specs=pl.BlockSpec((tm, tn), lambda i, j: (i, j)),
        out_shape=jax.ShapeDtypeStruct((m, n), BF16),
        compiler_params=_params("parallel", "arbitrary"),
        name="qkv_proj",
    )(xb, w_in_b, cos, sa, sb, q_gain, k_gain)


def _proj_kernel(x_ref, w_ref, o_ref, *, silu_from):
    j = pl.program_id(1)
    acc = _dot(x_ref[...], w_ref[...])

    @pl.when(j >= silu_from)
    def _():
        o_ref[...] = (acc * _sigmoid(acc)).astype(o_ref.dtype)

    @pl.when(j < silu_from)
    def _():
        o_ref[...] = acc.astype(o_ref.dtype)


def _proj(xb, wb, col0, ncols, silu_from, out_dtype, *, tm, tn, name):
    m, d = xb.shape
    off = col0 // tn
    return pl.pallas_call(
        functools.partial(_proj_kernel, silu_from=silu_from),
        grid=(m // tm, ncols // tn),
        in_specs=[pl.BlockSpec((tm, d), lambda i, j: (i, 0)),
                  pl.BlockSpec((d, tn), lambda i, j: (0, j + off))],
        out_specs=pl.BlockSpec((tm, tn), lambda i, j: (i, j)),
        out_shape=jax.ShapeDtypeStruct((m, ncols), out_dtype),
        compiler_params=_params("parallel", "arbitrary"),
        name=name,
    )(xb, wb)


def _attn_kernel(q_ref, k_ref, vt_ref, o_ref, m_ref, l_ref, acc_ref):
    ki = pl.program_id(3)

    @pl.when(ki == 0)
    def _():
        m_ref[...] = jnp.full_like(m_ref, -1e30)
        l_ref[...] = jnp.zeros_like(l_ref)
        acc_ref[...] = jnp.zeros_like(acc_ref)

    k = k_ref[...]
    vt = vt_ref[...]
    for g in range(Q_PER_KV):
        sl = slice(g * HEAD_DIM, (g + 1) * HEAD_DIM)
        row = slice(g, g + 1)
        st = lax.dot_general(k, q_ref[:, sl], NT_DIMS, preferred_element_type=F32)
        m_old = m_ref[row, :]
        m_new = jnp.maximum(m_old, jnp.max(st, axis=0, keepdims=True))
        alpha = jnp.exp(m_old - m_new)
        p = jnp.exp(st - m_new)
        l_ref[row, :] = alpha * l_ref[row, :] + jnp.sum(p, axis=0, keepdims=True)
        acc_ref[g] = alpha * acc_ref[g] + _dot(vt, p.astype(BF16))
        m_ref[row, :] = m_new

    @pl.when(ki == pl.num_programs(3) - 1)
    def _():
        for g in range(Q_PER_KV):
            o = acc_ref[g] / l_ref[g:g + 1, :]
            o_ref[:, g * HEAD_DIM:(g + 1) * HEAD_DIM] = o.T.astype(o_ref.dtype)


def _attention(qkv, vt, batch, seq, *, tq, tk):
    m = qkv.shape[0]
    gw = Q_PER_KV * HEAD_DIM
    nq, nk = seq // tq, seq // tk
    k_col0 = N_Q_HEADS * HEAD_DIM // HEAD_DIM
    return pl.pallas_call(
        _attn_kernel,
        grid=(batch, N_KV_HEADS, nq, nk),
        in_specs=[pl.BlockSpec((tq, gw), lambda b, h, qi, ki: (b * nq + qi, h)),
                  pl.BlockSpec((tk, HEAD_DIM), lambda b, h, qi, ki: (b * nk + ki, k_col0 + h)),
                  pl.BlockSpec((HEAD_DIM, tk), lambda b, h, qi, ki: (b * N_KV_HEADS + h, ki))],
        out_specs=pl.BlockSpec((tq, gw), lambda b, h, qi, ki: (b * nq + qi, h)),
        out_shape=jax.ShapeDtypeStruct((m, N_Q_HEADS * HEAD_DIM), BF16),
        scratch_shapes=[pltpu.VMEM((8, tq), F32), pltpu.VMEM((8, tq), F32),
                        pltpu.VMEM((Q_PER_KV, HEAD_DIM, tq), F32)],
        compiler_params=_params("parallel", "parallel", "parallel", "arbitrary"),
        name="gqa_attention",
    )(qkv, qkv, vt)


def _hgrn_kernel(*refs, reverse, final, nc, hb):
    if final:
        q_ref, z_ref, v_ref, lb_ref, tri_ref, prev_ref, g_ref, gain_ref, o_ref, st_ref = refs
    else:
        q_ref, z_ref, v_ref, lb_ref, tri_ref, o_ref, st_ref = refs
    C, D = HG_CHUNK, HG_DIM
    T, W = nc * C, hb * D

    @pl.when(pl.program_id(2) == 0)
    def _():
        st_ref[...] = jnp.zeros_like(st_ref)

    z = z_ref[...]
    lb = lb_ref[...]
    t = jnp.exp(-jnp.abs(z))
    r = 1.0 / (1.0 + t)
    pos = z >= 0
    sig = jnp.where(pos, r, t * r)
    sneg = jnp.where(pos, t * r, r)
    logf = jnp.log(lb + (1.0 - lb) * sig)
    key = (1.0 - lb) * sneg

    hi = logf.astype(BF16)
    r1 = logf - hi.astype(F32)
    mid = r1.astype(BF16)
    lo = (r1 - mid.astype(F32)).astype(BF16)
    cat = jnp.concatenate([hi, mid, lo], axis=1)
    tri = tri_ref[...]
    parts = []
    for s in range(T // TRI_BLOCK):
        bc = _dot(tri, cat[s * TRI_BLOCK:(s + 1) * TRI_BLOCK, :])
        parts.append((bc[:, :W] + bc[:, W:2 * W]) + bc[:, 2 * W:])
    b = parts[0] if len(parts) == 1 else jnp.concatenate(parts, axis=0)
    b3 = b.reshape(nc, C, W)
    ref_row = C // 2 if reverse else C // 2 - 1
    end_row = 0 if reverse else C - 1
    bref = b3[:, ref_row:ref_row + 1, :]
    btot = b3[:, end_row:end_row + 1, :]

    q3 = q_ref[...].astype(F32).reshape(nc, C, W)
    q_in = q3 * jnp.exp(b3 - bref)
    k_in = key.reshape(nc, C, W) * jnp.exp(bref - b3)
    q_st = (q_in * jnp.exp(bref)).astype(BF16)
    k_up = (k_in * jnp.exp(btot - bref)).astype(BF16)
    dec = jnp.exp(btot)
    q_in = q_in.astype(BF16)
    k_in = k_in.astype(BF16)
    v3 = v_ref[...].reshape(nc, C, W)

    rowi = lax.broadcasted_iota(jnp.int32, (C, C), 0)
    coli = lax.broadcasted_iota(jnp.int32, (C, C), 1)
    mask = (coli >= rowi) if reverse else (coli <= rowi)
    order = range(nc - 1, -1, -1) if reverse else range(nc)
    for h in range(hb):
        sl = slice(h * D, (h + 1) * D)
        state_t = st_ref[h]
        for c in order:
            vc = v3[c, :, sl]
            a = lax.dot_general(q_in[c, :, sl], k_in[c, :, sl], NT_DIMS, preferred_element_type=F32)
            a = jnp.where(mask, a, 0.0).astype(BF16)
            o = _dot(a, vc) + lax.dot_general(q_st[c, :, sl], state_t.astype(BF16), NT_DIMS,
                                              preferred_element_type=F32)
            upd_t = lax.dot_general(vc, k_up[c, :, sl], TN_DIMS, preferred_element_type=F32)
            state_t = state_t * dec[c, :, sl] + upd_t
            rows = slice(c * C, (c + 1) * C)
            if final:
                tot = o + prev_ref[rows, sl]
                ms = jnp.mean(tot * tot, axis=-1, keepdims=True)
                y = tot * lax.rsqrt(ms + RMS_EPS) * gain_ref[:, sl] * g_ref[rows, sl].astype(F32)
                o_ref[rows, sl] = y.astype(o_ref.dtype)
            else:
                o_ref[rows, sl] = o
        st_ref[h] = state_t


def _hgrn_pass(qh, z, vg, lb, tri, batch, seq, *, reverse, prev=None, gain=None, nc, hb):
    m, hv = qh.shape
    T, W = nc * HG_CHUNK, hb * HG_DIM
    nt, nh = seq // T, hv // W
    final = prev is not None

    def tmap(b, h, n):
        return b * nt + ((nt - 1 - n) if reverse else n)

    zoff = nh if reverse else 0
    in_specs = [pl.BlockSpec((T, W), lambda b, h, n: (tmap(b, h, n), h)),
                pl.BlockSpec((T, W), lambda b, h, n: (tmap(b, h, n), h + zoff)),
                pl.BlockSpec((T, W), lambda b, h, n: (tmap(b, h, n), h)),
                pl.BlockSpec((1, W), lambda b, h, n: (0, h)),
                pl.BlockSpec((TRI_BLOCK, TRI_BLOCK), lambda b, h, n: (0, 0))]
    args = [qh, z, vg, lb, tri]
    if final:
        in_specs += [pl.BlockSpec((T, W), lambda b, h, n: (tmap(b, h, n), h)),
                     pl.BlockSpec((T, W), lambda b, h, n: (tmap(b, h, n), h + nh)),
                     pl.BlockSpec((1, W), lambda b, h, n: (0, h))]
        args += [prev, vg, gain]
    return pl.pallas_call(
        functools.partial(_hgrn_kernel, reverse=reverse, final=final, nc=nc, hb=hb),
        grid=(batch, nh, nt),
        in_specs=in_specs,
        out_specs=pl.BlockSpec((T, W), lambda b, h, n: (tmap(b, h, n), h)),
        out_shape=jax.ShapeDtypeStruct((m, hv), BF16 if final else F32),
        scratch_shapes=[pltpu.VMEM((hb, HG_DIM, HG_DIM), F32)],
        compiler_params=_params("parallel", "parallel", "arbitrary"),
        name="hgrn2_bwd_merge" if final else "hgrn2_fwd",
    )(*args)


def _chunk_triangle(reverse):
    i = jnp.arange(TRI_BLOCK)
    same = (i[:, None] // HG_CHUNK) == (i[None, :] // HG_CHUNK)
    tri = (i[None, :] >= i[:, None]) if reverse else (i[None, :] <= i[:, None])
    return (same & tri).astype(BF16)


def _mix_kernel(x_ref, ya_ref, yh_ref, wga_ref, wgh_ref, wpa_ref, wpb_ref, bga_ref, bgh_ref, o_ref):
    x = x_ref[...]
    ga = _sigmoid(_dot(x, wga_ref[...]) + bga_ref[...])
    gh = _sigmoid(_dot(x, wgh_ref[...]) + bgh_ref[...])
    o = ga * _dot(ya_ref[...], wpa_ref[...]) + gh * _dot(yh_ref[...], wpb_ref[...])
    o_ref[...] = o.astype(o_ref.dtype)


def _mix(xb, ya, yh, w_gate_b, b_gate, w_pa_b, w_pb_b, *, tm, tn):
    m, d = xb.shape
    da, dh = ya.shape[1], yh.shape[1]
    nj = d // tn
    return pl.pallas_call(
        _mix_kernel,
        grid=(m // tm, nj),
        in_specs=[pl.BlockSpec((tm, d), lambda i, j: (i, 0)),
                  pl.BlockSpec((tm, da), lambda i, j: (i, 0)),
                  pl.BlockSpec((tm, dh), lambda i, j: (i, 0)),
                  pl.BlockSpec((d, tn), lambda i, j: (0, j)),
                  pl.BlockSpec((d, tn), lambda i, j: (0, j + nj)),
                  pl.BlockSpec((da, tn), lambda i, j: (0, j)),
                  pl.BlockSpec((dh, tn), lambda i, j: (0, j)),
                  pl.BlockSpec((1, tn), lambda i, j: (0, j)),
                  pl.BlockSpec((1, tn), lambda i, j: (0, j + nj))],
        out_specs=pl.BlockSpec((tm, tn), lambda i, j: (i, j)),
        out_shape=jax.ShapeDtypeStruct((m, d), BF16),
        compiler_params=_params("parallel", "arbitrary"),
        name="gated_branch_merge",
    )(xb, ya, yh, w_gate_b, w_gate_b, w_pa_b, w_pb_b, b_gate, b_gate)


def _mm_res_kernel(a_ref, w_ref, res_ref, o_ref, *, alpha):
    o_ref[...] = alpha * res_ref[...] + _dot(a_ref[...], w_ref[...])


def _mm_res(a, wb, res, alpha, *, tm, tn, name):
    m, k = a.shape
    n = wb.shape[1]
    return pl.pallas_call(
        functools.partial(_mm_res_kernel, alpha=alpha),
        grid=(m // tm, n // tn),
        in_specs=[pl.BlockSpec((tm, k), lambda i, j: (i, 0)),
                  pl.BlockSpec((k, tn), lambda i, j: (0, j)),
                  pl.BlockSpec((tm, tn), lambda i, j: (i, j))],
        out_specs=pl.BlockSpec((tm, tn), lambda i, j: (i, j)),
        out_shape=jax.ShapeDtypeStruct((m, n), F32),
        compiler_params=_params("parallel", "arbitrary"),
        name=name,
    )(a, wb, res)


def _ln_kernel(z_ref, g_ref, b_ref, o_ref, ob_ref):
    z = z_ref[...]
    mu = jnp.mean(z, axis=-1, keepdims=True)
    zc = z - mu
    var = jnp.mean(zc * zc, axis=-1, keepdims=True)
    y = zc * lax.rsqrt(var + LN_EPS) * g_ref[...] + b_ref[...]
    o_ref[...] = y
    ob_ref[...] = y.astype(ob_ref.dtype)


def _layer_norm(z, g, b, *, tr, name):
    m, d = z.shape
    row = pl.BlockSpec((tr, d), lambda i: (i, 0))
    vec = pl.BlockSpec((1, d), lambda i: (0, 0))
    return pl.pallas_call(
        _ln_kernel,
        grid=(m // tr,),
        in_specs=[row, vec, vec],
        out_specs=[row, row],
        out_shape=[jax.ShapeDtypeStruct((m, d), F32), jax.ShapeDtypeStruct((m, d), BF16)],
        compiler_params=_params("parallel"),
        name=name,
    )(z, g, b)


def _up_kernel(x_ref, wu_ref, wg_ref, u_ref, g_ref):
    x = x_ref[...]
    u_ref[...] = _dot(x, wu_ref[...]).astype(u_ref.dtype)
    g_ref[...] = _dot(x, wg_ref[...]).astype(g_ref.dtype)


def _ffn_up(xb, w_up_b, *, tm, tn):
    m, d = xb.shape
    dff = w_up_b.shape[1] // 2
    nj = dff // tn
    out = jax.ShapeDtypeStruct((m, dff), BF16)
    return pl.pallas_call(
        _up_kernel,
        grid=(m // tm, nj),
        in_specs=[pl.BlockSpec((tm, d), lambda i, j: (i, 0)),
                  pl.BlockSpec((d, tn), lambda i, j: (0, j)),
                  pl.BlockSpec((d, tn), lambda i, j: (0, j + nj))],
        out_specs=[pl.BlockSpec((tm, tn), lambda i, j: (i, j))] * 2,
        out_shape=[out, out],
        compiler_params=_params("parallel", "arbitrary"),
        name="ffn_up",
    )(xb, w_up_b, w_up_b)


def _conv_gate_kernel(g_ref, gp_ref, gn_ref, u_ref, w_ref, b_ref, o_ref, *, blocks_per_seq, halo):
    i = pl.program_id(0)
    g = g_ref[...].astype(F32)
    tr = g.shape[0]
    first = (i % blocks_per_seq) == 0
    last = (i % blocks_per_seq) == blocks_per_seq - 1
    prev_row = jnp.where(first, 0.0, gp_ref[halo - 1:halo, :].astype(F32))
    next_row = jnp.where(last, 0.0, gn_ref[0:1, :].astype(F32))
    ridx = lax.broadcasted_iota(jnp.int32, g.shape, 0)
    g_prev = jnp.where(ridx == 0, prev_row, pltpu.roll(g, 1, 0))
    g_next = jnp.where(ridx == tr - 1, next_row, pltpu.roll(g, tr - 1, 0))
    w = w_ref[...]
    gc = g_prev * w[0:1, :] + g * w[1:2, :] + g_next * w[2:3, :] + b_ref[...]
    o_ref[...] = (gc * _sigmoid(gc) * u_ref[...].astype(F32)).astype(o_ref.dtype)


def _conv_gate(u, gpre, conv_w, conv_b, seq, *, tr, tc):
    m, dff = u.shape
    halo = 16
    rb = tr // halo
    last_halo = m // halo - 1
    blk = pl.BlockSpec((tr, tc), lambda i, j: (i, j))
    return pl.pallas_call(
        functools.partial(_conv_gate_kernel, blocks_per_seq=seq // tr, halo=halo),
        grid=(m // tr, dff // tc),
        in_specs=[blk,
                  pl.BlockSpec((halo, tc), lambda i, j: (jnp.maximum(i * rb - 1, 0), j)),
                  pl.BlockSpec((halo, tc), lambda i, j: (jnp.minimum((i + 1) * rb, last_halo), j)),
                  blk,
                  pl.BlockSpec((3, tc), lambda i, j: (0, j)),
                  pl.BlockSpec((1, tc), lambda i, j: (0, j))],
        out_specs=blk,
        out_shape=jax.ShapeDtypeStruct((m, dff), BF16),
        compiler_params=_params("parallel", "parallel"),
        name="conv_gate",
    )(gpre, gpre, gpre, u, conv_w, conv_b)


def _ple_kernel(xb_ref, pb_ref, wpg_ref, wple_ref, x_ref, o_ref):
    gate = _sigmoid(_dot(xb_ref[...], wpg_ref[...]))
    o_ref[...] = x_ref[...] + gate * _dot(pb_ref[...], wple_ref[...])


def _ple(x2, x2b, pb, w_pg_b, w_ple_b, *, tm, tn):
    m, d = x2.shape
    dp = pb.shape[1]
    return pl.pallas_call(
        _ple_kernel,
        grid=(m // tm, d // tn),
        in_specs=[pl.BlockSpec((tm, d), lambda i, j: (i, 0)),
                  pl.BlockSpec((tm, dp), lambda i, j: (i, 0)),
                  pl.BlockSpec((d, tn), lambda i, j: (0, j)),
                  pl.BlockSpec((dp, tn), lambda i, j: (0, j)),
                  pl.BlockSpec((tm, tn), lambda i, j: (i, j))],
        out_specs=pl.BlockSpec((tm, tn), lambda i, j: (i, j)),
        out_shape=jax.ShapeDtypeStruct((m, d), F32),
        compiler_params=_params("parallel", "arbitrary"),
        name="ple_gate",
    )(x2b, pb, w_pg_b, w_ple_b, x2)


def _rope_tables(seq):
    pos = jnp.arange(seq)
    row = (pos // GRID_W).astype(F32)
    col = (pos % GRID_W).astype(F32)
    sec = HEAD_DIM // 2
    inv = ROPE_THETA ** (-jnp.arange(0, sec, 2, dtype=F32) / sec)
    ang_r = row[:, None] * inv[None, :]
    ang_c = col[:, None] * inv[None, :]
    ang = jnp.concatenate([ang_r, ang_r, ang_c, ang_c], axis=-1)
    cos, sin = jnp.cos(ang), jnp.sin(ang)
    first_half = (jnp.arange(HEAD_DIM) % sec) < sec // 2
    sa = jnp.where(first_half[None, :], -sin, 0.0)
    sb = jnp.where(first_half[None, :], 0.0, sin)
    return cos, sa, sb


def _layer(x, p, w_in, q_norm, k_norm, lb_f, lb_b, hg_norm, w_pa, w_pb, w_gate, b_gate, w_o,
           ln1_g, ln1_b, w_up, conv_w, conv_b, w_down, ln2_g, ln2_b, w_pg, w_ple,
           batch, seq, alpha, tables, tris):
    m, d = x.shape
    attn_dim = N_Q_HEADS * HEAD_DIM
    kv_dim = N_KV_HEADS * HEAD_DIM
    hq = HG_HEADS * HG_DIM
    row2 = lambda v: v.reshape(1, -1).astype(F32)
    bf = lambda v: v.astype(BF16)

    xb = bf(x)
    w_in_b = bf(w_in)
    cos, sa, sb = tables
    tm = min(1024, seq)

    qkv = _qkv_proj(xb, w_in_b, cos, sa, sb, row2(q_norm), row2(k_norm), seq, tm=tm, tn=512)
    c0 = attn_dim + 2 * kv_dim
    qh = _proj(xb, w_in_b, c0, hq, 0, BF16, tm=tm, tn=512, name="hgrn_q_proj")
    z = _proj(xb, w_in_b, c0 + hq, 2 * hq, 2 * hq // 512, F32, tm=tm, tn=512, name="hgrn_gate_proj")
    vg = _proj(xb, w_in_b, c0 + 3 * hq, 2 * hq, hq // 512, BF16, tm=tm, tn=512, name="hgrn_vg_proj")

    v = qkv[:, attn_dim + kv_dim:]
    vt = v.reshape(batch, seq, kv_dim).transpose(0, 2, 1).reshape(batch * kv_dim, seq)
    y_attn = _attention(qkv, vt, batch, seq, tq=min(512, seq), tk=min(1024, seq))

    nc = min(8, seq // HG_CHUNK)
    o_fwd = _hgrn_pass(qh, z, vg, row2(lb_f), tris[0], batch, seq, reverse=False, nc=nc, hb=2)
    y_hgrn = _hgrn_pass(qh, z, vg, row2(lb_b), tris[1], batch, seq, reverse=True,
                        prev=o_fwd, gain=row2(hg_norm), nc=nc, hb=2)

    mixed = _mix(xb, y_attn, y_hgrn, bf(w_gate), row2(b_gate), bf(w_pa), bf(w_pb), tm=min(512, seq), tn=512)
    z1 = _mm_res(mixed, bf(w_o), x, alpha, tm=tm, tn=512, name="attn_out_residual")
    x1, x1b = _layer_norm(z1, row2(ln1_g), row2(ln1_b), tr=256, name="layer_norm_1")

    u, gpre = _ffn_up(x1b, bf(w_up), tm=tm, tn=256)
    act = _conv_gate(u, gpre, conv_w.astype(F32), row2(conv_b), seq, tr=128, tc=u.shape[1] // 2)
    z2 = _mm_res(act, bf(w_down), x1, alpha, tm=min(512, seq), tn=512, name="ffn_down_residual")
    x2, x2b = _layer_norm(z2, row2(ln2_g), row2(ln2_b), tr=256, name="layer_norm_2")

    return _ple(x2, x2b, bf(p), bf(w_pg), bf(w_ple), tm=tm, tn=512)


def kernel(x, p, w_in, q_norm, k_norm, lb_logits, hg_norm, w_pa, w_pb, w_gate, b_gate, w_o, ln1_g, ln1_b,
           w_up, conv_w, conv_b, w_down, ln2_g, ln2_b, w_pg, w_ple):
    batch, seq, d = x.shape
    depth = w_in.shape[0]
    alpha = (2.0 * depth) ** 0.25
    tables = _rope_tables(seq)
    tris = (_chunk_triangle(False), _chunk_triangle(True))
    lb_all = jnp.cumsum(jax.nn.softmax(lb_logits.astype(F32), axis=1), axis=1)
    h = x.reshape(batch * seq, d)
    for i in range(depth):
        h = _layer(h, p[i].reshape(batch * seq, -1), w_in[i], q_norm[i], k_norm[i], lb_all[0, i], lb_all[1, i],
                   hg_norm[i], w_pa[i], w_pb[i], w_gate[i], b_gate[i], w_o[i], ln1_g[i], ln1_b[i],
                   w_up[i], conv_w[i], conv_b[i], w_down[i], ln2_g[i], ln2_b[i], w_pg[i], w_ple[i],
                   batch, seq, alpha, tables, tris)
    return h.reshape(batch, seq, d)
```

```python
import functools

import jax
import jax.numpy as jnp
from jax import lax
from jax.experimental import pallas as pl
from jax.experimental.pallas import tpu as pltpu

F32 = jnp.float32
BF16 = jnp.bfloat16

GRID_W = 64
HEAD_DIM = 128
N_Q_HEADS = 16
N_KV_HEADS = 4
Q_PER_KV = N_Q_HEADS // N_KV_HEADS
ROPE_THETA = 10000.0
HG_HEADS = 16
HG_DIM = 128
HG_CHUNK = 64
RMS_EPS = 1e-6
LN_EPS = 1e-5

LOG2_E = 1.4426950408889634
LANES = 128
TRI_BLOCK = 256
VMEM_LIMIT = 56 * 1024 * 1024

NT_DIMS = (((1,), (1,)), ((), ()))
TN_DIMS = (((0,), (0,)), ((), ()))


def _params(*sem):
    return pltpu.CompilerParams(dimension_semantics=sem, vmem_limit_bytes=VMEM_LIMIT)


def _dot(a, b):
    return jnp.dot(a, b, preferred_element_type=F32)


def _sigmoid(v):
    return 1.0 / (1.0 + jnp.exp(-v))


def _qkv_kernel(x_ref, w_ref, cos_ref, sa_ref, sb_ref, qg_ref, kg_ref, o_ref, *, nqb, nkb, q_scale):
    j = pl.program_id(1)
    acc = _dot(x_ref[...], w_ref[...])

    def norm_rope(gain_ref, scale):
        cos, sa, sb = cos_ref[...], sa_ref[...], sb_ref[...]
        gain = gain_ref[...]
        for h in range(acc.shape[1] // HEAD_DIM):
            sl = slice(h * HEAD_DIM, (h + 1) * HEAD_DIM)
            blk = acc[:, sl]
            ms = jnp.mean(blk * blk, axis=-1, keepdims=True)
            y = blk * lax.rsqrt(ms + RMS_EPS) * gain
            rot = (pltpu.roll(y, HEAD_DIM - HEAD_DIM // 4, 1) * sa
                   + pltpu.roll(y, HEAD_DIM // 4, 1) * sb)
            o_ref[:, sl] = ((y * cos + rot) * scale).astype(o_ref.dtype)

    @pl.when(j < nqb)
    def _():
        norm_rope(qg_ref, q_scale)

    @pl.when((j >= nqb) & (j < nqb + nkb))
    def _():
        norm_rope(kg_ref, 1.0)

    @pl.when(j >= nqb + nkb)
    def _():
        o_ref[...] = acc.astype(o_ref.dtype)


def _qkv_proj(xb, w_in_b, cos, sa, sb, q_gain, k_gain, seq, *, tm, tn):
    m, d = xb.shape
    attn_dim = N_Q_HEADS * HEAD_DIM
    kv_dim = N_KV_HEADS * HEAD_DIM
    n = attn_dim + 2 * kv_dim
    nsb = seq // tm
    tab = pl.BlockSpec((tm, HEAD_DIM), lambda i, j: (i % nsb, 0))
    vec = pl.BlockSpec((1, HEAD_DIM), lambda i, j: (0, 0))
    return pl.pallas_call(
        functools.partial(_qkv_kernel, nqb=attn_dim // tn, nkb=kv_dim // tn,
                          q_scale=HEAD_DIM ** -0.5 * LOG2_E),
        grid=(m // tm, n // tn),
        in_specs=[pl.BlockSpec((tm, d), lambda i, j: (i, 0)),
                  pl.BlockSpec((d, tn), lambda i, j: (0, j)),
                  tab, tab, tab, vec, vec],
        out_specs=pl.BlockSpec((tm, tn), lambda i, j: (i, j)),
        out_shape=jax.ShapeDtypeStruct((m, n), BF16),
        compiler_params=_params("parallel", "arbitrary"),
        name="qkv_proj",
    )(xb, w_in_b, cos, sa, sb, q_gain, k_gain)


def _proj_kernel(x_ref, w_ref, o_ref, *, silu_from):
    j = pl.program_id(1)
    acc = _dot(x_ref[...], w_ref[...])

    @pl.when(j >= silu_from)
    def _():
        o_ref[...] = (acc * _sigmoid(acc)).astype(o_ref.dtype)

    @pl.when(j < silu_from)
    def _():
        o_ref[...] = acc.astype(o_ref.dtype)


def _proj(xb, wb, col0, ncols, silu_from, out_dtype, *, tm, tn, name):
    m, d = xb.shape
    off = col0 // tn
    return pl.pallas_call(
        functools.partial(_proj_kernel, silu_from=silu_from),
        grid=(m // tm, ncols // tn),
        in_specs=[pl.BlockSpec((tm, d), lambda i, j: (i, 0)),
                  pl.BlockSpec((d, tn), lambda i, j: (0, j + off))],
        out_specs=pl.BlockSpec((tm, tn), lambda i, j: (i, j)),
        out_shape=jax.ShapeDtypeStruct((m, ncols), out_dtype),
        compiler_params=_params("parallel", "arbitrary"),
        name=name,
    )(xb, wb)


def _attn_kernel(q_ref, k_ref, vt_ref, o_ref, m_ref, l_ref, acc_ref):
    ki = pl.program_id(3)

    @pl.when(ki == 0)
    def _():
        m_ref[...] = jnp.full_like(m_ref, -1e30)
        l_ref[...] = jnp.zeros_like(l_ref)
        acc_ref[...] = jnp.zeros_like(acc_ref)

    k = k_ref[...]
    vt = vt_ref[...]

    def scores_t(g):
        return lax.dot_general(k, q_ref[:, g * HEAD_DIM:(g + 1) * HEAD_DIM], NT_DIMS,
                               preferred_element_type=F32)

    st_next = scores_t(0)
    for g in range(Q_PER_KV):
        row = slice(g, g + 1)
        st = st_next
        if g + 1 < Q_PER_KV:
            st_next = scores_t(g + 1)
        m_old = m_ref[row, :]
        m_new = jnp.maximum(m_old, jnp.max(st, axis=0, keepdims=True))
        alpha = jnp.exp2(m_old - m_new)
        p = jnp.exp2(st - m_new)
        l_ref[row, :] = alpha * l_ref[row, :] + jnp.sum(p, axis=0, keepdims=True)
        acc_ref[g] = alpha * acc_ref[g] + _dot(vt, p.astype(BF16))
        m_ref[row, :] = m_new

    @pl.when(ki == pl.num_programs(3) - 1)
    def _():
        for g in range(Q_PER_KV):
            o = acc_ref[g] / l_ref[g:g + 1, :]
            o_ref[:, g * HEAD_DIM:(g + 1) * HEAD_DIM] = o.T.astype(o_ref.dtype)


def _attention(qkv, vt, batch, seq, *, tq, tk):
    m = qkv.shape[0]
    gw = Q_PER_KV * HEAD_DIM
    nq, nk = seq // tq, seq // tk
    k_col0 = N_Q_HEADS
    return pl.pallas_call(
        _attn_kernel,
        grid=(batch, N_KV_HEADS, nq, nk),
        in_specs=[pl.BlockSpec((tq, gw), lambda b, h, qi, ki: (b * nq + qi, h)),
                  pl.BlockSpec((tk, HEAD_DIM), lambda b, h, qi, ki: (b * nk + ki, k_col0 + h)),
                  pl.BlockSpec((HEAD_DIM, tk), lambda b, h, qi, ki: (b * N_KV_HEADS + h, ki))],
        out_specs=pl.BlockSpec((tq, gw), lambda b, h, qi, ki: (b * nq + qi, h)),
        out_shape=jax.ShapeDtypeStruct((m, N_Q_HEADS * HEAD_DIM), BF16),
        scratch_shapes=[pltpu.VMEM((8, tq), F32), pltpu.VMEM((8, tq), F32),
                        pltpu.VMEM((Q_PER_KV, HEAD_DIM, tq), F32)],
        compiler_params=_params("parallel", "parallel", "parallel", "arbitrary"),
        name="gqa_attention",
    )(qkv, qkv, vt)


def _hgrn_kernel(*refs, reverse, final, nc, hb):
    if final:
        q_ref, z_ref, v_ref, lb_ref, tri_ref, prev_ref, g_ref, gain_ref, o_ref, st_ref = refs
    else:
        q_ref, z_ref, v_ref, lb_ref, tri_ref, o_ref, st_ref = refs
    C, D = HG_CHUNK, HG_DIM
    T, W = nc * C, hb * D

    @pl.when(pl.program_id(2) == 0)
    def _():
        st_ref[...] = jnp.zeros_like(st_ref)

    z = z_ref[...]
    lb = lb_ref[...]
    t = jnp.exp(-jnp.abs(z))
    r = 1.0 / (1.0 + t)
    pos = z >= 0
    sig = jnp.where(pos, r, t * r)
    sneg = jnp.where(pos, t * r, r)
    logf = jnp.log(lb + (1.0 - lb) * sig)
    key = (1.0 - lb) * sneg

    hi = logf.astype(BF16)
    r1 = logf - hi.astype(F32)
    mid = r1.astype(BF16)
    lo = (r1 - mid.astype(F32)).astype(BF16)
    cat = jnp.concatenate([hi, mid, lo], axis=1)
    tri = tri_ref[...]
    parts = []
    for s in range(T // TRI_BLOCK):
        bc = _dot(tri, cat[s * TRI_BLOCK:(s + 1) * TRI_BLOCK, :])
        parts.append((bc[:, :W] + bc[:, W:2 * W]) + bc[:, 2 * W:])
    b = parts[0] if len(parts) == 1 else jnp.concatenate(parts, axis=0)
    b3 = b.reshape(nc, C, W)
    ref_row = C // 2 if reverse else C // 2 - 1
    end_row = 0 if reverse else C - 1
    bref = b3[:, ref_row:ref_row + 1, :]
    btot = b3[:, end_row:end_row + 1, :]

    q3 = q_ref[...].astype(F32).reshape(nc, C, W)
    q_in = q3 * jnp.exp(b3 - bref)
    k_in = key.reshape(nc, C, W) * jnp.exp(bref - b3)
    q_st = (q_in * jnp.exp(bref)).astype(BF16).reshape(T, W)
    k_up = (k_in * jnp.exp(btot - bref)).astype(BF16).reshape(T, W)
    dec = jnp.exp(btot)
    q_in = q_in.astype(BF16).reshape(T, W)
    k_in = k_in.astype(BF16).reshape(T, W)

    P = 2 * C
    rowi = lax.broadcasted_iota(jnp.int32, (P, P), 0)
    coli = lax.broadcasted_iota(jnp.int32, (P, P), 1)
    mask = ((rowi >= C) == (coli >= C)) & ((coli >= rowi) if reverse else (coli <= rowi))
    order = range(nc - 1, -1, -1) if reverse else range(nc)
    for h in range(hb):
        sl = slice(h * D, (h + 1) * D)
        scores = []
        for pr in range(nc // 2):
            rows = slice(pr * P, (pr + 1) * P)
            a = lax.dot_general(q_in[rows, sl], k_in[rows, sl], NT_DIMS, preferred_element_type=F32)
            scores.append(jnp.where(mask, a, 0.0).astype(BF16))
        upd_t = [lax.dot_general(v_ref[c * C:(c + 1) * C, sl], k_up[c * C:(c + 1) * C, sl], TN_DIMS,
                                 preferred_element_type=F32) for c in range(nc)]
        state_t = st_ref[h]
        entering = [None] * nc
        for c in order:
            entering[c] = state_t.astype(BF16)
            state_t = state_t * dec[c, :, sl] + upd_t[c]
        st_ref[h] = state_t
        for pr in range(nc // 2):
            rows = slice(pr * P, (pr + 1) * P)
            o_pair = _dot(scores[pr], v_ref[rows, sl])
            for half in range(2):
                c = 2 * pr + half
                rows_c = slice(c * C, (c + 1) * C)
                o = o_pair[half * C:(half + 1) * C] + lax.dot_general(
                    q_st[rows_c, sl], entering[c], NT_DIMS, preferred_element_type=F32)
                if final:
                    tot = o + prev_ref[rows_c, sl]
                    ms = jnp.mean(tot * tot, axis=-1, keepdims=True)
                    y = tot * lax.rsqrt(ms + RMS_EPS) * gain_ref[:, sl] * g_ref[rows_c, sl].astype(F32)
                    o_ref[rows_c, sl] = y.astype(o_ref.dtype)
                else:
                    o_ref[rows_c, sl] = o


def _hgrn_pass(qh, z, vg, lb, tri, batch, seq, *, reverse, prev=None, gain=None, nc, hb):
    m, hv = qh.shape
    T, W = nc * HG_CHUNK, hb * HG_DIM
    nt, nh = seq // T, hv // W
    final = prev is not None

    def tmap(b, h, n):
        return b * nt + ((nt - 1 - n) if reverse else n)

    zoff = nh if reverse else 0
    in_specs = [pl.BlockSpec((T, W), lambda b, h, n: (tmap(b, h, n), h)),
                pl.BlockSpec((T, W), lambda b, h, n: (tmap(b, h, n), h + zoff)),
                pl.BlockSpec((T, W), lambda b, h, n: (tmap(b, h, n), h)),
                pl.BlockSpec((1, W), lambda b, h, n: (0, h)),
                pl.BlockSpec((TRI_BLOCK, TRI_BLOCK), lambda b, h, n: (0, 0))]
    args = [qh, z, vg, lb, tri]
    if final:
        in_specs += [pl.BlockSpec((T, W), lambda b, h, n: (tmap(b, h, n), h)),
                     pl.BlockSpec((T, W), lambda b, h, n: (tmap(b, h, n), h + nh)),
                     pl.BlockSpec((1, W), lambda b, h, n: (0, h))]
        args += [prev, vg, gain]
    return pl.pallas_call(
        functools.partial(_hgrn_kernel, reverse=reverse, final=final, nc=nc, hb=hb),
        grid=(batch, nh, nt),
        in_specs=in_specs,
        out_specs=pl.BlockSpec((T, W), lambda b, h, n: (tmap(b, h, n), h)),
        out_shape=jax.ShapeDtypeStruct((m, hv), BF16 if final else F32),
        scratch_shapes=[pltpu.VMEM((hb, HG_DIM, HG_DIM), F32)],
        compiler_params=_params("parallel", "parallel", "arbitrary"),
        name="hgrn2_bwd_merge" if final else "hgrn2_fwd",
    )(*args)


def _chunk_triangle(reverse):
    i = jnp.arange(TRI_BLOCK)
    same = (i[:, None] // HG_CHUNK) == (i[None, :] // HG_CHUNK)
    tri = (i[None, :] >= i[:, None]) if reverse else (i[None, :] <= i[:, None])
    return (same & tri).astype(BF16)


def _mix_kernel(x_ref, ya_ref, yh_ref, wga_ref, wgh_ref, wpa_ref, wpb_ref, bga_ref, bgh_ref, o_ref):
    x = x_ref[...]
    ga = _sigmoid(_dot(x, wga_ref[...]) + bga_ref[...])
    gh = _sigmoid(_dot(x, wgh_ref[...]) + bgh_ref[...])
    o = ga * _dot(ya_ref[...], wpa_ref[...]) + gh * _dot(yh_ref[...], wpb_ref[...])
    o_ref[...] = o.astype(o_ref.dtype)


def _mix(xb, ya, yh, w_gate_b, b_gate, w_pa_b, w_pb_b, *, tm, tn):
    m, d = xb.shape
    da, dh = ya.shape[1], yh.shape[1]
    nj = d // tn
    return pl.pallas_call(
        _mix_kernel,
        grid=(m // tm, nj),
        in_specs=[pl.BlockSpec((tm, d), lambda i, j: (i, 0)),
                  pl.BlockSpec((tm, da), lambda i, j: (i, 0)),
                  pl.BlockSpec((tm, dh), lambda i, j: (i, 0)),
                  pl.BlockSpec((d, tn), lambda i, j: (0, j)),
                  pl.BlockSpec((d, tn), lambda i, j: (0, j + nj)),
                  pl.BlockSpec((da, tn), lambda i, j: (0, j)),
                  pl.BlockSpec((dh, tn), lambda i, j: (0, j)),
                  pl.BlockSpec((1, tn), lambda i, j: (0, j)),
                  pl.BlockSpec((1, tn), lambda i, j: (0, j + nj))],
        out_specs=pl.BlockSpec((tm, tn), lambda i, j: (i, j)),
        out_shape=jax.ShapeDtypeStruct((m, d), BF16),
        compiler_params=_params("parallel", "arbitrary"),
        name="gated_branch_merge",
    )(xb, ya, yh, w_gate_b, w_gate_b, w_pa_b, w_pb_b, b_gate, b_gate)


def _mm_res_kernel(a_ref, w_ref, res_ref, o_ref, *, alpha):
    o_ref[...] = alpha * res_ref[...] + _dot(a_ref[...], w_ref[...])


def _mm_res(a, wb, res3, alpha, *, tm, name):
    m, k = a.shape
    nj, _, tn = res3.shape
    return pl.pallas_call(
        functools.partial(_mm_res_kernel, alpha=alpha),
        grid=(m // tm, nj),
        in_specs=[pl.BlockSpec((tm, k), lambda i, j: (i, 0)),
                  pl.BlockSpec((k, tn), lambda i, j: (0, j)),
                  pl.BlockSpec((None, tm, tn), lambda i, j: (j, i, 0))],
        out_specs=pl.BlockSpec((tm, tn), lambda i, j: (i, j)),
        out_shape=jax.ShapeDtypeStruct((m, nj * tn), F32),
        compiler_params=_params("parallel", "arbitrary"),
        name=name,
    )(a, wb, res3)


def _row_stats(chunks, width):
    mu = sum(jnp.sum(c, axis=-1, keepdims=True) for c in chunks) / width
    var = sum(jnp.sum((c - mu) * (c - mu), axis=-1, keepdims=True) for c in chunks) / width
    return mu, lax.rsqrt(var + LN_EPS)


def _mm_res_ln_kernel(a_ref, w_ref, res_ref, g_ref, b_ref, o_ref, ob_ref, *, alpha):
    j = pl.program_id(1)
    nj, _, tn = o_ref.shape
    o_ref[j] = alpha * res_ref[...] + _dot(a_ref[...], w_ref[...])

    @pl.when(j == nj - 1)
    def _():
        mu, inv = _row_stats([o_ref[c] for c in range(nj)], nj * tn)
        for c in range(nj):
            cols = slice(c * tn, (c + 1) * tn)
            y = (o_ref[c] - mu) * inv * g_ref[:, cols] + b_ref[:, cols]
            o_ref[c] = y
            ob_ref[:, cols] = y.astype(ob_ref.dtype)


def _mm_res_ln(a, wb, res, alpha, g, b, *, tm, tn, name):
    m, k = a.shape
    n = wb.shape[1]
    nj = n // tn
    vec = pl.BlockSpec((1, n), lambda i, j: (0, 0))
    return pl.pallas_call(
        functools.partial(_mm_res_ln_kernel, alpha=alpha),
        grid=(m // tm, nj),
        in_specs=[pl.BlockSpec((tm, k), lambda i, j: (i, 0)),
                  pl.BlockSpec((k, tn), lambda i, j: (0, j)),
                  pl.BlockSpec((tm, tn), lambda i, j: (i, j)),
                  vec, vec],
        out_specs=[pl.BlockSpec((nj, tm, tn), lambda i, j: (0, i, 0)),
                   pl.BlockSpec((tm, n), lambda i, j: (i, 0))],
        out_shape=[jax.ShapeDtypeStruct((nj, m, tn), F32), jax.ShapeDtypeStruct((m, n), BF16)],
        compiler_params=_params("parallel", "arbitrary"),
        name=name,
    )(a, wb, res, g, b)


def _ffn_up_kernel(x_ref, xp_ref, xn_ref, wu_ref, wg_ref, cw_ref, cb_ref, o_ref, xh_ref, g_ref, *, blocks_per_seq):
    i, j = pl.program_id(0), pl.program_id(1)
    tm = x_ref.shape[0]
    halo = xp_ref.shape[0]

    @pl.when(j == 0)
    def _():
        first = (i % blocks_per_seq) == 0
        last = (i % blocks_per_seq) == blocks_per_seq - 1
        xh_ref[0:halo] = jnp.where(first, jnp.zeros_like(xp_ref), xp_ref[...])
        xh_ref[halo:halo + tm] = x_ref[...]
        xh_ref[halo + tm:] = jnp.where(last, jnp.zeros_like(xn_ref), xn_ref[...])

    parts = g_ref.shape[0]
    rows = tm // parts
    w = cw_ref[...]

    def matmuls(s):
        g_ref[s] = _dot(xh_ref[s * rows:(s + 1) * rows + 2 * halo], wg_ref[...])
        return _dot(x_ref[s * rows:(s + 1) * rows], wu_ref[...])

    def gate(s, u):
        win = g_ref[s, halo - 8:halo + rows + 8]
        g_prev = pltpu.roll(win, 1, 0)[8:8 + rows]
        g_next = pltpu.roll(win, rows + 15, 0)[8:8 + rows]
        gc = g_prev * w[0:1] + win[8:8 + rows] * w[1:2] + g_next * w[2:3] + cb_ref[...]
        o_ref[s * rows:(s + 1) * rows] = (gc * _sigmoid(gc) * u).astype(o_ref.dtype)

    u_prev = matmuls(0)
    for s in range(1, parts):
        u_next = matmuls(s)
        gate(s - 1, u_prev)
        u_prev = u_next
    gate(parts - 1, u_prev)


def _ffn_up(xb, w_up_b, conv_w, conv_b, seq, *, tm, tn, parts):
    m, d = xb.shape
    dff = w_up_b.shape[1] // 2
    nj = dff // tn
    halo = 16
    rb = tm // halo
    last_halo = m // halo - 1
    return pl.pallas_call(
        functools.partial(_ffn_up_kernel, blocks_per_seq=seq // tm),
        grid=(m // tm, nj),
        in_specs=[pl.BlockSpec((tm, d), lambda i, j: (i, 0)),
                  pl.BlockSpec((halo, d), lambda i, j: (jnp.maximum(i * rb - 1, 0), 0)),
                  pl.BlockSpec((halo, d), lambda i, j: (jnp.minimum((i + 1) * rb, last_halo), 0)),
                  pl.BlockSpec((d, tn), lambda i, j: (0, j)),
                  pl.BlockSpec((d, tn), lambda i, j: (0, j + nj)),
                  pl.BlockSpec((3, tn), lambda i, j: (0, j)),
                  pl.BlockSpec((1, tn), lambda i, j: (0, j))],
        out_specs=pl.BlockSpec((tm, tn), lambda i, j: (i, j)),
        out_shape=jax.ShapeDtypeStruct((m, dff), BF16),
        scratch_shapes=[pltpu.VMEM((tm + 2 * halo, d), BF16),
                        pltpu.VMEM((parts, tm // parts + 2 * halo, tn), F32)],
        compiler_params=_params("parallel", "arbitrary"),
        name="ffn_up_conv_gate",
    )(xb, xb, xb, w_up_b, w_up_b, conv_w, conv_b)


def _ln_ple_kernel(z_ref, g_ref, b_ref, pb_ref, wpg_ref, wple_ref, o_ref, x_ref, xb_ref):
    j = pl.program_id(1)
    nj, _, tn = x_ref.shape

    @pl.when(j == 0)
    def _():
        col = lambda c: slice(c * tn, (c + 1) * tn)
        mu, inv = _row_stats([z_ref[:, col(c)] for c in range(nj)], nj * tn)
        for c in range(nj):
            y = (z_ref[:, col(c)] - mu) * inv * g_ref[:, col(c)] + b_ref[:, col(c)]
            x_ref[c] = y
            xb_ref[:, col(c)] = y.astype(xb_ref.dtype)

    gate = _sigmoid(_dot(xb_ref[...], wpg_ref[...]))
    o_ref[...] = x_ref[j] + gate * _dot(pb_ref[...], wple_ref[...])


def _ln_ple(z, g, b, pb, w_pg_b, w_ple_b, *, tm, tn):
    m, d = z.shape
    dp = pb.shape[1]
    vec = pl.BlockSpec((1, d), lambda i, j: (0, 0))
    return pl.pallas_call(
        _ln_ple_kernel,
        grid=(m // tm, d // tn),
        in_specs=[pl.BlockSpec((tm, d), lambda i, j: (i, 0)),
                  vec, vec,
                  pl.BlockSpec((tm, dp), lambda i, j: (i, 0)),
                  pl.BlockSpec((d, tn), lambda i, j: (0, j)),
                  pl.BlockSpec((dp, tn), lambda i, j: (0, j))],
        out_specs=pl.BlockSpec((tm, tn), lambda i, j: (i, j)),
        out_shape=jax.ShapeDtypeStruct((m, d), F32),
        scratch_shapes=[pltpu.VMEM((d // tn, tm, tn), F32), pltpu.VMEM((tm, d), BF16)],
        compiler_params=_params("parallel", "arbitrary"),
        name="ln_ple_gate",
    )(z, g, b, pb, w_pg_b, w_ple_b)


def _rope_tables(seq):
    pos = jnp.arange(seq)
    row = (pos // GRID_W).astype(F32)
    col = (pos % GRID_W).astype(F32)
    sec = HEAD_DIM // 2
    inv = ROPE_THETA ** (-jnp.arange(0, sec, 2, dtype=F32) / sec)
    ang_r = row[:, None] * inv[None, :]
    ang_c = col[:, None] * inv[None, :]
    ang = jnp.concatenate([ang_r, ang_r, ang_c, ang_c], axis=-1)
    cos, sin = jnp.cos(ang), jnp.sin(ang)
    first_half = (jnp.arange(HEAD_DIM) % sec) < sec // 2
    sa = jnp.where(first_half[None, :], -sin, 0.0)
    sb = jnp.where(first_half[None, :], 0.0, sin)
    return cos, sa, sb


def _layer(x, p, w_in, q_norm, k_norm, lb_f, lb_b, hg_norm, w_pa, w_pb, w_gate, b_gate, w_o,
           ln1_g, ln1_b, w_up, conv_w, conv_b, w_down, ln2_g, ln2_b, w_pg, w_ple,
           batch, seq, alpha, tables, tris):
    m, d = x.shape
    attn_dim = N_Q_HEADS * HEAD_DIM
    kv_dim = N_KV_HEADS * HEAD_DIM
    hq = HG_HEADS * HG_DIM
    row2 = lambda v: v.reshape(1, -1).astype(F32)
    bf = lambda v: v.astype(BF16)

    xb = bf(x)
    w_in_b = bf(w_in)
    cos, sa, sb = tables
    tm = min(1024, seq)

    qkv = _qkv_proj(xb, w_in_b, cos, sa, sb, row2(q_norm), row2(k_norm), seq, tm=tm, tn=512)
    c0 = attn_dim + 2 * kv_dim
    qh = _proj(xb, w_in_b, c0, hq, 0, BF16, tm=tm, tn=512, name="hgrn_q_proj")
    z = _proj(xb, w_in_b, c0 + hq, 2 * hq, 2 * hq // 512, F32, tm=tm, tn=512, name="hgrn_gate_proj")
    vg = _proj(xb, w_in_b, c0 + 3 * hq, 2 * hq, hq // 512, BF16, tm=tm, tn=512, name="hgrn_vg_proj")

    v = qkv[:, attn_dim + kv_dim:]
    vt = v.reshape(batch, seq, kv_dim).transpose(0, 2, 1).reshape(batch * kv_dim, seq)
    y_attn = _attention(qkv, vt, batch, seq, tq=min(512, seq), tk=min(2048, seq))

    nc = min(8, seq // HG_CHUNK)
    o_fwd = _hgrn_pass(qh, z, vg, row2(lb_f), tris[0], batch, seq, reverse=False, nc=nc, hb=4)
    y_hgrn = _hgrn_pass(qh, z, vg, row2(lb_b), tris[1], batch, seq, reverse=True,
                        prev=o_fwd, gain=row2(hg_norm), nc=nc, hb=4)

    mixed = _mix(xb, y_attn, y_hgrn, bf(w_gate), row2(b_gate), bf(w_pa), bf(w_pb), tm=min(512, seq), tn=512)
    x1_blocked, x1b = _mm_res_ln(mixed, bf(w_o), x, alpha, row2(ln1_g), row2(ln1_b),
                                 tm=min(512, seq), tn=512, name="attn_out_residual_ln")

    act = _ffn_up(x1b, bf(w_up), conv_w.astype(F32), row2(conv_b), seq, tm=tm, tn=256, parts=2)
    z2 = _mm_res(act, bf(w_down), x1_blocked, alpha, tm=min(512, seq), name="ffn_down_residual")

    return _ln_ple(z2, row2(ln2_g), row2(ln2_b), bf(p), bf(w_pg), bf(w_ple), tm=min(512, seq), tn=512)


def kernel(x, p, w_in, q_norm, k_norm, lb_logits, hg_norm, w_pa, w_pb, w_gate, b_gate, w_o, ln1_g, ln1_b,
           w_up, conv_w, conv_b, w_down, ln2_g, ln2_b, w_pg, w_ple):
    batch, seq, d = x.shape
    depth = w_in.shape[0]
    alpha = (2.0 * depth) ** 0.25
    tables = _rope_tables(seq)
    tris = (_chunk_triangle(False), _chunk_triangle(True))
    lb_all = jnp.cumsum(jax.nn.softmax(lb_logits.astype(F32), axis=1), axis=1)
    h = x.reshape(batch * seq, d)
    for i in range(depth):
        h = _layer(h, p[i].reshape(batch * seq, -1), w_in[i], q_norm[i], k_norm[i], lb_all[0, i], lb_all[1, i],
                   hg_norm[i], w_pa[i], w_pb[i], w_gate[i], b_gate[i], w_o[i], ln1_g[i], ln1_b[i],
                   w_up[i], conv_w[i], conv_b[i], w_down[i], ln2_g[i], ln2_b[i], w_pg[i], w_ple[i],
                   batch, seq, alpha, tables, tris)
    return h.reshape(batch, seq, d)
```

```python
import functools

import jax
import jax.numpy as jnp
from jax import lax
from jax.experimental import pallas as pl
from jax.experimental.pallas import tpu as pltpu

F32 = jnp.float32
BF16 = jnp.bfloat16

GRID_W = 64
HEAD_DIM = 128
N_Q_HEADS = 16
N_KV_HEADS = 4
Q_PER_KV = N_Q_HEADS // N_KV_HEADS
ROPE_THETA = 10000.0
HG_HEADS = 16
HG_DIM = 128
HG_CHUNK = 64
RMS_EPS = 1e-6
LN_EPS = 1e-5

LOG2_E = 1.4426950408889634
LANES = 128
TRI_BLOCK = 256
VMEM_LIMIT = 56 * 1024 * 1024

NT_DIMS = (((1,), (1,)), ((), ()))
TN_DIMS = (((0,), (0,)), ((), ()))


def _params(*sem):
    return pltpu.CompilerParams(dimension_semantics=sem, vmem_limit_bytes=VMEM_LIMIT)


def _dot(a, b):
    return jnp.dot(a, b, preferred_element_type=F32)


def _sigmoid(v):
    return 0.5 + 0.5 * jnp.tanh(0.5 * v)


def _silu(v):
    h = 0.5 * v
    return h + h * jnp.tanh(h)


def _qkv_kernel(x_ref, w_ref, cos_ref, sa_ref, sb_ref, qg_ref, kg_ref, o_ref, *, nqb, nkb, q_scale):
    j = pl.program_id(1)
    acc = _dot(x_ref[...], w_ref[...])

    def norm_rope(gain_ref, scale):
        cos, sa, sb = cos_ref[...], sa_ref[...], sb_ref[...]
        gain = gain_ref[...]
        for h in range(acc.shape[1] // HEAD_DIM):
            sl = slice(h * HEAD_DIM, (h + 1) * HEAD_DIM)
            blk = acc[:, sl]
            ms = jnp.mean(blk * blk, axis=-1, keepdims=True)
            y = blk * lax.rsqrt(ms + RMS_EPS) * gain
            rot = (pltpu.roll(y, HEAD_DIM - HEAD_DIM // 4, 1) * sa
                   + pltpu.roll(y, HEAD_DIM // 4, 1) * sb)
            o_ref[:, sl] = ((y * cos + rot) * scale).astype(o_ref.dtype)

    @pl.when(j < nqb)
    def _():
        norm_rope(qg_ref, q_scale)

    @pl.when((j >= nqb) & (j < nqb + nkb))
    def _():
        norm_rope(kg_ref, 1.0)

    @pl.when(j >= nqb + nkb)
    def _():
        o_ref[...] = acc.astype(o_ref.dtype)


def _qkv_proj(xb, w_in_b, cos, sa, sb, q_gain, k_gain, seq, *, tm, tn):
    m, d = xb.shape
    attn_dim = N_Q_HEADS * HEAD_DIM
    kv_dim = N_KV_HEADS * HEAD_DIM
    n = attn_dim + 2 * kv_dim
    nsb = seq // tm
    tab = pl.BlockSpec((tm, HEAD_DIM), lambda i, j: (i % nsb, 0))
    vec = pl.BlockSpec((1, HEAD_DIM), lambda i, j: (0, 0))
    return pl.pallas_call(
        functools.partial(_qkv_kernel, nqb=attn_dim // tn, nkb=kv_dim // tn,
                          q_scale=HEAD_DIM ** -0.5 * LOG2_E),
        grid=(m // tm, n // tn),
        in_specs=[pl.BlockSpec((tm, d), lambda i, j: (i, 0)),
                  pl.BlockSpec((d, tn), lambda i, j: (0, j)),
                  tab, tab, tab, vec, vec],
        out_specs=pl.BlockSpec((tm, tn), lambda i, j: (i, j)),
        out_shape=jax.ShapeDtypeStruct((m, n), BF16),
        compiler_params=_params("parallel", "arbitrary"),
        name="qkv_proj",
    )(xb, w_in_b, cos, sa, sb, q_gain, k_gain)


def _proj_kernel(x_ref, w_ref, o_ref, *, silu_from):
    j = pl.program_id(1)
    acc = _dot(x_ref[...], w_ref[...])

    @pl.when(j >= silu_from)
    def _():
        o_ref[...] = _silu(acc).astype(o_ref.dtype)

    @pl.when(j < silu_from)
    def _():
        o_ref[...] = acc.astype(o_ref.dtype)


def _proj(xb, wb, col0, ncols, silu_from, out_dtype, *, tm, tn, name):
    m, d = xb.shape
    off = col0 // tn
    return pl.pallas_call(
        functools.partial(_proj_kernel, silu_from=silu_from),
        grid=(m // tm, ncols // tn),
        in_specs=[pl.BlockSpec((tm, d), lambda i, j: (i, 0)),
                  pl.BlockSpec((d, tn), lambda i, j: (0, j + off))],
        out_specs=pl.BlockSpec((tm, tn), lambda i, j: (i, j)),
        out_shape=jax.ShapeDtypeStruct((m, ncols), out_dtype),
        compiler_params=_params("parallel", "arbitrary"),
        name=name,
    )(xb, wb)


def _attn_kernel(*refs, n_cast):
    q_ref, k_ref, vt_ref = refs[:3]
    w_refs = refs[3:3 + n_cast]
    o_ref = refs[3 + n_cast]
    wb_refs = refs[4 + n_cast:4 + 2 * n_cast]
    m_ref, l_ref, acc_ref = refs[4 + 2 * n_cast:]
    ki = pl.program_id(3)

    for w_ref, wb_ref in zip(w_refs, wb_refs):
        wb_ref[...] = w_ref[...].astype(wb_ref.dtype)

    @pl.when(ki == 0)
    def _():
        m_ref[...] = jnp.full_like(m_ref, -1e30)
        l_ref[...] = jnp.zeros_like(l_ref)
        acc_ref[...] = jnp.zeros_like(acc_ref)

    k = k_ref[...]
    vt = vt_ref[...]

    def scores_t(g):
        return lax.dot_general(k, q_ref[:, g * HEAD_DIM:(g + 1) * HEAD_DIM], NT_DIMS,
                               preferred_element_type=F32)

    st_next = scores_t(0)
    for g in range(Q_PER_KV):
        row = slice(g, g + 1)
        st = st_next
        if g + 1 < Q_PER_KV:
            st_next = scores_t(g + 1)
        m_old = m_ref[row, :]
        m_new = jnp.maximum(m_old, jnp.max(st, axis=0, keepdims=True))
        alpha = jnp.exp2(m_old - m_new)
        p = jnp.exp2(st - m_new)
        l_ref[row, :] = alpha * l_ref[row, :] + jnp.sum(p, axis=0, keepdims=True)
        acc_ref[g] = alpha * acc_ref[g] + _dot(vt, p.astype(BF16))
        m_ref[row, :] = m_new

    @pl.when(ki == pl.num_programs(3) - 1)
    def _():
        for g in range(Q_PER_KV):
            o = acc_ref[g] / l_ref[g:g + 1, :]
            o_ref[:, g * HEAD_DIM:(g + 1) * HEAD_DIM] = o.T.astype(o_ref.dtype)


def _attention(qkv, vt, weights, batch, seq, *, tq, tk):
    m = qkv.shape[0]
    gw = Q_PER_KV * HEAD_DIM
    nq, nk = seq // tq, seq // tk
    k_col0 = N_Q_HEADS
    slab_rows = 16
    rows = batch * N_KV_HEADS * nq * nk * slab_rows
    rides = [w.size % (rows * LANES) == 0 for w in weights]
    flat = [w.reshape(rows, w.size // rows) for w, r in zip(weights, rides) if r]
    step = lambda b, h, qi, ki: (((b * N_KV_HEADS + h) * nq + qi) * nk + ki, 0)
    slabs = [pl.BlockSpec((slab_rows, f.shape[1]), step) for f in flat]
    out = pl.pallas_call(
        functools.partial(_attn_kernel, n_cast=len(flat)),
        grid=(batch, N_KV_HEADS, nq, nk),
        in_specs=[pl.BlockSpec((tq, gw), lambda b, h, qi, ki: (b * nq + qi, h)),
                  pl.BlockSpec((tk, HEAD_DIM), lambda b, h, qi, ki: (b * nk + ki, k_col0 + h)),
                  pl.BlockSpec((HEAD_DIM, tk), lambda b, h, qi, ki: (b * N_KV_HEADS + h, ki))] + slabs,
        out_specs=[pl.BlockSpec((tq, gw), lambda b, h, qi, ki: (b * nq + qi, h))] + slabs,
        out_shape=[jax.ShapeDtypeStruct((m, N_Q_HEADS * HEAD_DIM), BF16)]
        + [jax.ShapeDtypeStruct(f.shape, BF16) for f in flat],
        scratch_shapes=[pltpu.VMEM((8, tq), F32), pltpu.VMEM((8, tq), F32),
                        pltpu.VMEM((Q_PER_KV, HEAD_DIM, tq), F32)],
        compiler_params=_params("parallel", "parallel", "parallel", "arbitrary"),
        name="gqa_attention",
    )(qkv, qkv, vt, *flat)
    cast = iter(out[1:])
    return out[0], [next(cast).reshape(w.shape) if r else w.astype(BF16) for w, r in zip(weights, rides)]


def _hgrn_kernel(*refs, reverse, final, nc, hb):
    if final:
        q_ref, z_ref, v_ref, lb_ref, tri_ref, prev_ref, g_ref, gain_ref, o_ref, st_ref = refs
    else:
        q_ref, z_ref, v_ref, lb_ref, tri_ref, o_ref, st_ref = refs
    C, D = HG_CHUNK, HG_DIM
    T, W = nc * C, hb * D

    @pl.when(pl.program_id(2) == 0)
    def _():
        st_ref[...] = jnp.zeros_like(st_ref)

    lb = lb_ref[...]
    half_span = 0.5 * (1.0 - lb)
    g = half_span * jnp.tanh(0.5 * z_ref[...])
    key = half_span - g
    logf = jnp.log2(0.5 * (1.0 + lb) + g)

    hi = logf.astype(BF16)
    r1 = logf - hi.astype(F32)
    mid = r1.astype(BF16)
    lo = (r1 - mid.astype(F32)).astype(BF16)
    cat = jnp.concatenate([hi, mid, lo], axis=1)
    tri = tri_ref[...]
    parts = []
    for s in range(T // TRI_BLOCK):
        bc = _dot(tri, cat[s * TRI_BLOCK:(s + 1) * TRI_BLOCK, :])
        parts.append((bc[:, :W] + bc[:, W:2 * W]) + bc[:, 2 * W:])
    b = parts[0] if len(parts) == 1 else jnp.concatenate(parts, axis=0)
    b3 = b.reshape(nc, C, W)
    ref_row = C // 2 if reverse else C // 2 - 1
    end_row = 0 if reverse else C - 1
    bref = b3[:, ref_row:ref_row + 1, :]
    btot = b3[:, end_row:end_row + 1, :]

    q3 = q_ref[...].astype(F32).reshape(nc, C, W)
    q_in = q3 * jnp.exp2(b3 - bref)
    k_in = key.reshape(nc, C, W) * jnp.exp2(bref - b3)
    q_st = (q_in * jnp.exp2(bref)).astype(BF16).reshape(T, W)
    k_up = (k_in * jnp.exp2(btot - bref)).astype(BF16).reshape(T, W)
    dec = jnp.exp2(btot)
    q_in = q_in.astype(BF16).reshape(T, W)
    k_in = k_in.astype(BF16).reshape(T, W)

    P = 2 * C
    rowi = lax.broadcasted_iota(jnp.int32, (P, P), 0)
    coli = lax.broadcasted_iota(jnp.int32, (P, P), 1)
    mask = ((rowi >= C) == (coli >= C)) & ((coli >= rowi) if reverse else (coli <= rowi))
    order = range(nc - 1, -1, -1) if reverse else range(nc)
    for h in range(hb):
        sl = slice(h * D, (h + 1) * D)
        scores = []
        for pr in range(nc // 2):
            rows = slice(pr * P, (pr + 1) * P)
            a = lax.dot_general(q_in[rows, sl], k_in[rows, sl], NT_DIMS, preferred_element_type=F32)
            scores.append(jnp.where(mask, a, 0.0).astype(BF16))
        upd_t = [lax.dot_general(v_ref[c * C:(c + 1) * C, sl], k_up[c * C:(c + 1) * C, sl], TN_DIMS,
                                 preferred_element_type=F32) for c in range(nc)]
        state_t = st_ref[h]
        entering = [None] * nc
        for c in order:
            entering[c] = state_t.astype(BF16)
            state_t = state_t * dec[c, :, sl] + upd_t[c]
        st_ref[h] = state_t
        for pr in range(nc // 2):
            rows = slice(pr * P, (pr + 1) * P)
            o_pair = _dot(scores[pr], v_ref[rows, sl])
            for half in range(2):
                c = 2 * pr + half
                rows_c = slice(c * C, (c + 1) * C)
                o = o_pair[half * C:(half + 1) * C] + lax.dot_general(
                    q_st[rows_c, sl], entering[c], NT_DIMS, preferred_element_type=F32)
                if final:
                    tot = o + prev_ref[rows_c, sl]
                    ms = jnp.mean(tot * tot, axis=-1, keepdims=True)
                    y = tot * lax.rsqrt(ms + RMS_EPS) * gain_ref[:, sl] * g_ref[rows_c, sl].astype(F32)
                    o_ref[rows_c, sl] = y.astype(o_ref.dtype)
                else:
                    o_ref[rows_c, sl] = o


def _hgrn_pass(qh, z, vg, lb, tri, batch, seq, *, reverse, prev=None, gain=None, nc, hb):
    m, hv = qh.shape
    T, W = nc * HG_CHUNK, hb * HG_DIM
    nt, nh = seq // T, hv // W
    final = prev is not None

    def tmap(b, h, n):
        return b * nt + ((nt - 1 - n) if reverse else n)

    zoff = nh if reverse else 0
    in_specs = [pl.BlockSpec((T, W), lambda b, h, n: (tmap(b, h, n), h)),
                pl.BlockSpec((T, W), lambda b, h, n: (tmap(b, h, n), h + zoff)),
                pl.BlockSpec((T, W), lambda b, h, n: (tmap(b, h, n), h)),
                pl.BlockSpec((1, W), lambda b, h, n: (0, h)),
                pl.BlockSpec((TRI_BLOCK, TRI_BLOCK), lambda b, h, n: (0, 0))]
    args = [qh, z, vg, lb, tri]
    if final:
        in_specs += [pl.BlockSpec((T, W), lambda b, h, n: (tmap(b, h, n), h)),
                     pl.BlockSpec((T, W), lambda b, h, n: (tmap(b, h, n), h + nh)),
                     pl.BlockSpec((1, W), lambda b, h, n: (0, h))]
        args += [prev, vg, gain]
    return pl.pallas_call(
        functools.partial(_hgrn_kernel, reverse=reverse, final=final, nc=nc, hb=hb),
        grid=(batch, nh, nt),
        in_specs=in_specs,
        out_specs=pl.BlockSpec((T, W), lambda b, h, n: (tmap(b, h, n), h)),
        out_shape=jax.ShapeDtypeStruct((m, hv), BF16 if final else F32),
        scratch_shapes=[pltpu.VMEM((hb, HG_DIM, HG_DIM), F32)],
        compiler_params=_params("parallel", "parallel", "arbitrary"),
        name="hgrn2_bwd_merge" if final else "hgrn2_fwd",
    )(*args)


def _chunk_triangle(reverse):
    i = jnp.arange(TRI_BLOCK)
    same = (i[:, None] // HG_CHUNK) == (i[None, :] // HG_CHUNK)
    tri = (i[None, :] >= i[:, None]) if reverse else (i[None, :] <= i[:, None])
    return (same & tri).astype(BF16)


def _mix_kernel(x_ref, ya_ref, yh_ref, wga_ref, wgh_ref, wpa_ref, wpb_ref, bga_ref, bgh_ref, o_ref):
    x = x_ref[...]
    ga = _sigmoid(_dot(x, wga_ref[...]) + bga_ref[...])
    gh = _sigmoid(_dot(x, wgh_ref[...]) + bgh_ref[...])
    o = ga * _dot(ya_ref[...], wpa_ref[...]) + gh * _dot(yh_ref[...], wpb_ref[...])
    o_ref[...] = o.astype(o_ref.dtype)


def _mix(xb, ya, yh, w_gate_b, b_gate, w_pa_b, w_pb_b, *, tm, tn):
    m, d = xb.shape
    da, dh = ya.shape[1], yh.shape[1]
    nj = d // tn
    return pl.pallas_call(
        _mix_kernel,
        grid=(m // tm, nj),
        in_specs=[pl.BlockSpec((tm, d), lambda i, j: (i, 0)),
                  pl.BlockSpec((tm, da), lambda i, j: (i, 0)),
                  pl.BlockSpec((tm, dh), lambda i, j: (i, 0)),
                  pl.BlockSpec((d, tn), lambda i, j: (0, j)),
                  pl.BlockSpec((d, tn), lambda i, j: (0, j + nj)),
                  pl.BlockSpec((da, tn), lambda i, j: (0, j)),
                  pl.BlockSpec((dh, tn), lambda i, j: (0, j)),
                  pl.BlockSpec((1, tn), lambda i, j: (0, j)),
                  pl.BlockSpec((1, tn), lambda i, j: (0, j + nj))],
        out_specs=pl.BlockSpec((tm, tn), lambda i, j: (i, j)),
        out_shape=jax.ShapeDtypeStruct((m, d), BF16),
        compiler_params=_params("parallel", "arbitrary"),
        name="gated_branch_merge",
    )(xb, ya, yh, w_gate_b, w_gate_b, w_pa_b, w_pb_b, b_gate, b_gate)


def _mm_res_kernel(a_ref, w_ref, res_ref, o_ref, *, alpha):
    o_ref[...] = alpha * res_ref[...] + _dot(a_ref[...], w_ref[...])


def _mm_res(a, wb, res3, alpha, *, tm, name):
    m, k = a.shape
    nj, _, tn = res3.shape
    return pl.pallas_call(
        functools.partial(_mm_res_kernel, alpha=alpha),
        grid=(m // tm, nj),
        in_specs=[pl.BlockSpec((tm, k), lambda i, j: (i, 0)),
                  pl.BlockSpec((k, tn), lambda i, j: (0, j)),
                  pl.BlockSpec((None, tm, tn), lambda i, j: (j, i, 0))],
        out_specs=pl.BlockSpec((tm, tn), lambda i, j: (i, j)),
        out_shape=jax.ShapeDtypeStruct((m, nj * tn), F32),
        compiler_params=_params("parallel", "arbitrary"),
        name=name,
    )(a, wb, res3)


def _row_stats(load, groups, width):
    def row_sum(fn):
        acc = fn(load(0))
        for k in range(1, groups):
            acc = acc + fn(load(k))
        return jnp.sum(acc, axis=-1, keepdims=True)

    mu = row_sum(lambda t: t) / width
    var = row_sum(lambda t: (t - mu) * (t - mu)) / width
    return mu, lax.rsqrt(var + LN_EPS)


def _mm_res_ln_kernel(a_ref, w_ref, res_ref, g_ref, b_ref, o_ref, ob_ref, *, alpha):
    j = pl.program_id(1)
    nj, _, tn = o_ref.shape
    o_ref[j] = alpha * res_ref[...] + _dot(a_ref[...], w_ref[...])

    @pl.when(j == nj - 1)
    def _():
        per = tn // LANES
        load = lambda k: o_ref[k // per, :, (k % per) * LANES:(k % per + 1) * LANES]
        mu, inv = _row_stats(load, nj * per, nj * tn)
        for c in range(nj):
            cols = slice(c * tn, (c + 1) * tn)
            y = (o_ref[c] - mu) * inv * g_ref[:, cols] + b_ref[:, cols]
            o_ref[c] = y
            ob_ref[:, cols] = y.astype(ob_ref.dtype)


def _mm_res_ln(a, wb, res, alpha, g, b, *, tm, tn, name):
    m, k = a.shape
    n = wb.shape[1]
    nj = n // tn
    vec = pl.BlockSpec((1, n), lambda i, j: (0, 0))
    return pl.pallas_call(
        functools.partial(_mm_res_ln_kernel, alpha=alpha),
        grid=(m // tm, nj),
        in_specs=[pl.BlockSpec((tm, k), lambda i, j: (i, 0)),
                  pl.BlockSpec((k, tn), lambda i, j: (0, j)),
                  pl.BlockSpec((tm, tn), lambda i, j: (i, j)),
                  vec, vec],
        out_specs=[pl.BlockSpec((nj, tm, tn), lambda i, j: (0, i, 0)),
                   pl.BlockSpec((tm, n), lambda i, j: (i, 0))],
        out_shape=[jax.ShapeDtypeStruct((nj, m, tn), F32), jax.ShapeDtypeStruct((m, n), BF16)],
        compiler_params=_params("parallel", "arbitrary"),
        name=name,
    )(a, wb, res, g, b)


def _ffn_up_kernel(x_ref, xp_ref, xn_ref, wu_ref, wg_ref, cw_ref, cb_ref, o_ref, xh_ref, g_ref, *, blocks_per_seq):
    i, j = pl.program_id(0), pl.program_id(1)
    tm = x_ref.shape[0]
    halo = xp_ref.shape[0]

    @pl.when(j == 0)
    def _():
        first = (i % blocks_per_seq) == 0
        last = (i % blocks_per_seq) == blocks_per_seq - 1
        xh_ref[0:halo] = jnp.where(first, jnp.zeros_like(xp_ref), xp_ref[...])
        xh_ref[halo:halo + tm] = x_ref[...]
        xh_ref[halo + tm:] = jnp.where(last, jnp.zeros_like(xn_ref), xn_ref[...])

    parts = g_ref.shape[0]
    rows = tm // parts
    w = cw_ref[...]

    def matmuls(s):
        g_ref[s] = _dot(xh_ref[s * rows:(s + 1) * rows + 2 * halo], wg_ref[...])
        return _dot(xh_ref[halo + s * rows:halo + (s + 1) * rows], wu_ref[...])

    def gate(s, u):
        win = g_ref[s, halo - 8:halo + rows + 8]
        g_prev = pltpu.roll(win, 1, 0)[8:8 + rows]
        g_next = pltpu.roll(win, rows + 15, 0)[8:8 + rows]
        gc = g_prev * w[0:1] + win[8:8 + rows] * w[1:2] + g_next * w[2:3] + cb_ref[...]
        o_ref[s * rows:(s + 1) * rows] = (_silu(gc) * u).astype(o_ref.dtype)

    u_prev = matmuls(0)
    for s in range(1, parts):
        u_next = matmuls(s)
        gate(s - 1, u_prev)
        u_prev = u_next
    gate(parts - 1, u_prev)


def _ffn_up(xb, w_up_b, conv_w, conv_b, seq, *, tm, tn, parts):
    m, d = xb.shape
    dff = w_up_b.shape[1] // 2
    nj = dff // tn
    halo = 16
    rb = tm // halo
    last_halo = m // halo - 1
    return pl.pallas_call(
        functools.partial(_ffn_up_kernel, blocks_per_seq=seq // tm),
        grid=(m // tm, nj),
        in_specs=[pl.BlockSpec((tm, d), lambda i, j: (i, 0), pipeline_mode=pl.Buffered(1)),
                  pl.BlockSpec((halo, d), lambda i, j: (jnp.maximum(i * rb - 1, 0), 0)),
                  pl.BlockSpec((halo, d), lambda i, j: (jnp.minimum((i + 1) * rb, last_halo), 0)),
                  pl.BlockSpec((d, tn), lambda i, j: (0, j)),
                  pl.BlockSpec((d, tn), lambda i, j: (0, j + nj)),
                  pl.BlockSpec((3, tn), lambda i, j: (0, j)),
                  pl.BlockSpec((1, tn), lambda i, j: (0, j))],
        out_specs=pl.BlockSpec((tm, tn), lambda i, j: (i, j)),
        out_shape=jax.ShapeDtypeStruct((m, dff), BF16),
        scratch_shapes=[pltpu.VMEM((tm + 2 * halo, d), BF16),
                        pltpu.VMEM((parts, tm // parts + 2 * halo, tn), F32)],
        compiler_params=_params("parallel", "arbitrary"),
        name="ffn_up_conv_gate",
    )(xb, xb, xb, w_up_b, w_up_b, conv_w, conv_b)


def _ln_ple_kernel(z_ref, g_ref, b_ref, pb_ref, wpg_ref, wple_ref, o_ref, x_ref, xb_ref):
    j = pl.program_id(1)
    nj, _, tn = x_ref.shape

    @pl.when(j == 0)
    def _():
        col = lambda c: slice(c * tn, (c + 1) * tn)
        load = lambda k: z_ref[:, k * LANES:(k + 1) * LANES]
        mu, inv = _row_stats(load, nj * tn // LANES, nj * tn)
        for c in range(nj):
            y = (z_ref[:, col(c)] - mu) * inv * g_ref[:, col(c)] + b_ref[:, col(c)]
            x_ref[c] = y
            xb_ref[:, col(c)] = y.astype(xb_ref.dtype)

    gate = _sigmoid(_dot(xb_ref[...], wpg_ref[...]))
    o_ref[...] = x_ref[j] + gate * _dot(pb_ref[...], wple_ref[...])


def _ln_ple(z, g, b, pb, w_pg_b, w_ple_b, *, tm, tn):
    m, d = z.shape
    dp = pb.shape[1]
    vec = pl.BlockSpec((1, d), lambda i, j: (0, 0))
    return pl.pallas_call(
        _ln_ple_kernel,
        grid=(m // tm, d // tn),
        in_specs=[pl.BlockSpec((tm, d), lambda i, j: (i, 0)),
                  vec, vec,
                  pl.BlockSpec((tm, dp), lambda i, j: (i, 0)),
                  pl.BlockSpec((d, tn), lambda i, j: (0, j)),
                  pl.BlockSpec((dp, tn), lambda i, j: (0, j))],
        out_specs=pl.BlockSpec((tm, tn), lambda i, j: (i, j)),
        out_shape=jax.ShapeDtypeStruct((m, d), F32),
        scratch_shapes=[pltpu.VMEM((d // tn, tm, tn), F32), pltpu.VMEM((tm, d), BF16)],
        compiler_params=_params("parallel", "arbitrary"),
        name="ln_ple_gate",
    )(z, g, b, pb, w_pg_b, w_ple_b)


def _rope_tables(seq):
    pos = jnp.arange(seq)
    row = (pos // GRID_W).astype(F32)
    col = (pos % GRID_W).astype(F32)
    sec = HEAD_DIM // 2
    inv = ROPE_THETA ** (-jnp.arange(0, sec, 2, dtype=F32) / sec)
    ang_r = row[:, None] * inv[None, :]
    ang_c = col[:, None] * inv[None, :]
    ang = jnp.concatenate([ang_r, ang_r, ang_c, ang_c], axis=-1)
    cos, sin = jnp.cos(ang), jnp.sin(ang)
    first_half = (jnp.arange(HEAD_DIM) % sec) < sec // 2
    sa = jnp.where(first_half[None, :], -sin, 0.0)
    sb = jnp.where(first_half[None, :], 0.0, sin)
    return cos, sa, sb


def _layer(x, p, w_in, q_norm, k_norm, lb_f, lb_b, hg_norm, w_pa, w_pb, w_gate, b_gate, w_o,
           ln1_g, ln1_b, w_up, conv_w, conv_b, w_down, ln2_g, ln2_b, w_pg, w_ple,
           batch, seq, alpha, tables, tris):
    m, d = x.shape
    attn_dim = N_Q_HEADS * HEAD_DIM
    kv_dim = N_KV_HEADS * HEAD_DIM
    hq = HG_HEADS * HG_DIM
    row2 = lambda v: v.reshape(1, -1).astype(F32)
    bf = lambda v: v.astype(BF16)

    xb = bf(x)
    w_in_b = bf(w_in)
    cos, sa, sb = tables
    tm = min(1024, seq)

    qkv = _qkv_proj(xb, w_in_b, cos, sa, sb, row2(q_norm), row2(k_norm), seq, tm=tm, tn=512)
    c0 = attn_dim + 2 * kv_dim
    qh = _proj(xb, w_in_b, c0, hq, 0, BF16, tm=tm, tn=512, name="hgrn_q_proj")
    z = _proj(xb, w_in_b, c0 + hq, 2 * hq, 2 * hq // 512, F32, tm=tm, tn=512, name="hgrn_gate_proj")
    vg = _proj(xb, w_in_b, c0 + 3 * hq, 2 * hq, hq // 512, BF16, tm=tm, tn=512, name="hgrn_vg_proj")

    v = qkv[:, attn_dim + kv_dim:]
    vt = v.reshape(batch, seq, kv_dim).transpose(0, 2, 1).reshape(batch * kv_dim, seq)
    later_weights = [w_gate, w_pa, w_pb, w_o, w_up, w_down, w_pg, w_ple]
    y_attn, (w_gate_b, w_pa_b, w_pb_b, w_o_b, w_up_b, w_down_b, w_pg_b, w_ple_b) = _attention(
        qkv, vt, later_weights, batch, seq, tq=min(1024, seq), tk=min(2048, seq))

    nc = min(8, seq // HG_CHUNK)
    o_fwd = _hgrn_pass(qh, z, vg, row2(lb_f), tris[0], batch, seq, reverse=False, nc=nc, hb=4)
    y_hgrn = _hgrn_pass(qh, z, vg, row2(lb_b), tris[1], batch, seq, reverse=True,
                        prev=o_fwd, gain=row2(hg_norm), nc=nc, hb=4)

    mixed = _mix(xb, y_attn, y_hgrn, w_gate_b, row2(b_gate), w_pa_b, w_pb_b, tm=min(512, seq), tn=512)
    x1_blocked, x1b = _mm_res_ln(mixed, w_o_b, x, alpha, row2(ln1_g), row2(ln1_b),
                                 tm=min(512, seq), tn=512, name="attn_out_residual_ln")

    act = _ffn_up(x1b, w_up_b, conv_w.astype(F32), row2(conv_b), seq, tm=min(2048, seq), tn=256, parts=4)
    z2 = _mm_res(act, w_down_b, x1_blocked, alpha, tm=min(512, seq), name="ffn_down_residual")

    return _ln_ple(z2, row2(ln2_g), row2(ln2_b), bf(p), w_pg_b, w_ple_b, tm=min(512, seq), tn=512)


def kernel(x, p, w_in, q_norm, k_norm, lb_logits, hg_norm, w_pa, w_pb, w_gate, b_gate, w_o, ln1_g, ln1_b,
           w_up, conv_w, conv_b, w_down, ln2_g, ln2_b, w_pg, w_ple):
    batch, seq, d = x.shape
    depth = w_in.shape[0]
    alpha = (2.0 * depth) ** 0.25
    tables = _rope_tables(seq)
    tris = (_chunk_triangle(False), _chunk_triangle(True))
    lb_all = jnp.cumsum(jax.nn.softmax(lb_logits.astype(F32), axis=1), axis=1)
    h = x.reshape(batch * seq, d)
    for i in range(depth):
        h = _layer(h, p[i].reshape(batch * seq, -1), w_in[i], q_norm[i], k_norm[i], lb_all[0, i], lb_all[1, i],
                   hg_norm[i], w_pa[i], w_pb[i], w_gate[i], b_gate[i], w_o[i], ln1_g[i], ln1_b[i],
                   w_up[i], conv_w[i], conv_b[i], w_down[i], ln2_g[i], ln2_b[i], w_pg[i], w_ple[i],
                   batch, seq, alpha, tables, tris)
    return h.reshape(batch, seq, d)
```

```python
import functools

import jax
import jax.numpy as jnp
from jax import lax
from jax.experimental import pallas as pl
from jax.experimental.pallas import tpu as pltpu

F32 = jnp.float32
BF16 = jnp.bfloat16

GRID_W = 64
HEAD_DIM = 128
N_Q_HEADS = 16
N_KV_HEADS = 4
Q_PER_KV = N_Q_HEADS // N_KV_HEADS
ROPE_THETA = 10000.0
HG_HEADS = 16
HG_DIM = 128
HG_CHUNK = 64
RMS_EPS = 1e-6
LN_EPS = 1e-5

LOG2_E = 1.4426950408889634
LANES = 128
TRI_BLOCK = 256
LN_ROW_BLOCK = 64
VMEM_LIMIT = 56 * 1024 * 1024

NT_DIMS = (((1,), (1,)), ((), ()))
TN_DIMS = (((0,), (0,)), ((), ()))


def _params(*sem):
    return pltpu.CompilerParams(dimension_semantics=sem, vmem_limit_bytes=VMEM_LIMIT)


def _dot(a, b):
    return jnp.dot(a, b, preferred_element_type=F32)


def _sigmoid(v):
    return 0.5 + 0.5 * jnp.tanh(0.5 * v)


def _silu(v):
    h = 0.5 * v
    return h + h * jnp.tanh(h)


def _qkv_kernel(x_ref, w_ref, cos_ref, sa_ref, sb_ref, qg_ref, kg_ref, o_ref, *, nqb, nkb, q_scale):
    j = pl.program_id(1)
    acc = _dot(x_ref[...], w_ref[...])

    def norm_rope(gain_ref, scale):
        cos, sa, sb = cos_ref[...], sa_ref[...], sb_ref[...]
        gain = gain_ref[...]
        for h in range(acc.shape[1] // HEAD_DIM):
            sl = slice(h * HEAD_DIM, (h + 1) * HEAD_DIM)
            blk = acc[:, sl]
            ms = jnp.mean(blk * blk, axis=-1, keepdims=True)
            y = blk * lax.rsqrt(ms + RMS_EPS) * gain
            rot = (pltpu.roll(y, HEAD_DIM - HEAD_DIM // 4, 1) * sa
                   + pltpu.roll(y, HEAD_DIM // 4, 1) * sb)
            o_ref[:, sl] = ((y * cos + rot) * scale).astype(o_ref.dtype)

    @pl.when(j < nqb)
    def _():
        norm_rope(qg_ref, q_scale)

    @pl.when((j >= nqb) & (j < nqb + nkb))
    def _():
        norm_rope(kg_ref, 1.0)

    @pl.when(j >= nqb + nkb)
    def _():
        o_ref[...] = acc.astype(o_ref.dtype)


def _qkv_proj(xb, w_in_b, cos, sa, sb, q_gain, k_gain, seq, *, tm, tn):
    m, d = xb.shape
    attn_dim = N_Q_HEADS * HEAD_DIM
    kv_dim = N_KV_HEADS * HEAD_DIM
    n = attn_dim + 2 * kv_dim
    nsb = seq // tm
    tab = pl.BlockSpec((tm, HEAD_DIM), lambda i, j: (i % nsb, 0))
    vec = pl.BlockSpec((1, HEAD_DIM), lambda i, j: (0, 0))
    return pl.pallas_call(
        functools.partial(_qkv_kernel, nqb=attn_dim // tn, nkb=kv_dim // tn,
                          q_scale=HEAD_DIM ** -0.5 * LOG2_E),
        grid=(m // tm, n // tn),
        in_specs=[pl.BlockSpec((tm, d), lambda i, j: (i, 0)),
                  pl.BlockSpec((d, tn), lambda i, j: (0, j)),
                  tab, tab, tab, vec, vec],
        out_specs=pl.BlockSpec((tm, tn), lambda i, j: (i, j)),
        out_shape=jax.ShapeDtypeStruct((m, n), BF16),
        compiler_params=_params("parallel", "arbitrary"),
        name="qkv_proj",
    )(xb, w_in_b, cos, sa, sb, q_gain, k_gain)


def _proj_kernel(x_ref, w_ref, o_ref, *, silu_from):
    j = pl.program_id(1)
    acc = _dot(x_ref[...], w_ref[...])

    @pl.when(j >= silu_from)
    def _():
        o_ref[...] = _silu(acc).astype(o_ref.dtype)

    @pl.when(j < silu_from)
    def _():
        o_ref[...] = acc.astype(o_ref.dtype)


def _proj(xb, wb, col0, ncols, silu_from, out_dtype, *, tm, tn, name):
    m, d = xb.shape
    off = col0 // tn
    return pl.pallas_call(
        functools.partial(_proj_kernel, silu_from=silu_from),
        grid=(m // tm, ncols // tn),
        in_specs=[pl.BlockSpec((tm, d), lambda i, j: (i, 0)),
                  pl.BlockSpec((d, tn), lambda i, j: (0, j + off))],
        out_specs=pl.BlockSpec((tm, tn), lambda i, j: (i, j)),
        out_shape=jax.ShapeDtypeStruct((m, ncols), out_dtype),
        compiler_params=_params("parallel", "arbitrary"),
        name=name,
    )(xb, wb)


def _attn_kernel(*refs, n_cast):
    q_ref, k_ref, vt_ref = refs[:3]
    w_refs = refs[3:3 + n_cast]
    o_ref = refs[3 + n_cast]
    wb_refs = refs[4 + n_cast:4 + 2 * n_cast]
    m_ref, l_ref, acc_ref = refs[4 + 2 * n_cast:]
    ki = pl.program_id(3)

    for w_ref, wb_ref in zip(w_refs, wb_refs):
        wb_ref[...] = w_ref[...].astype(wb_ref.dtype)

    @pl.when(ki == 0)
    def _():
        m_ref[...] = jnp.full_like(m_ref, -1e30)
        l_ref[...] = jnp.zeros_like(l_ref)
        acc_ref[...] = jnp.zeros_like(acc_ref)

    k = k_ref[...]
    vt = vt_ref[...]

    def scores_t(g):
        return lax.dot_general(k, q_ref[:, g * HEAD_DIM:(g + 1) * HEAD_DIM], NT_DIMS,
                               preferred_element_type=F32)

    st_next = scores_t(0)
    for g in range(Q_PER_KV):
        row = slice(g, g + 1)
        st = st_next
        if g + 1 < Q_PER_KV:
            st_next = scores_t(g + 1)
        m_old = m_ref[row, :]
        m_new = jnp.maximum(m_old, jnp.max(st, axis=0, keepdims=True))
        alpha = jnp.exp2(m_old - m_new)
        p = jnp.exp2(st - m_new)
        l_ref[row, :] = alpha * l_ref[row, :] + jnp.sum(p, axis=0, keepdims=True)
        acc_ref[g] = alpha * acc_ref[g] + _dot(vt, p.astype(BF16))
        m_ref[row, :] = m_new

    @pl.when(ki == pl.num_programs(3) - 1)
    def _():
        for g in range(Q_PER_KV):
            o = acc_ref[g] / l_ref[g:g + 1, :]
            o_ref[:, g * HEAD_DIM:(g + 1) * HEAD_DIM] = o.T.astype(o_ref.dtype)


def _attention(qkv, vt, weights, batch, seq, *, tq, tk):
    m = qkv.shape[0]
    gw = Q_PER_KV * HEAD_DIM
    nq, nk = seq // tq, seq // tk
    k_col0 = N_Q_HEADS
    steps = batch * N_KV_HEADS * nq * nk
    tile = 16

    def slab_spec(w):
        ntiles = w.shape[0] // tile
        if w.shape[0] % tile or (ntiles % steps and steps % ntiles):
            return None
        per_step, hold = max(ntiles // steps, 1), max(steps // ntiles, 1)
        return pl.BlockSpec((tile * per_step, w.shape[1]),
                            lambda b, h, qi, ki: ((((b * N_KV_HEADS + h) * nq + qi) * nk + ki) // hold, 0))

    specs = [slab_spec(w) for w in weights]
    flat = [w for w, s in zip(weights, specs) if s is not None]
    slabs = [s for s in specs if s is not None]
    rides = [s is not None for s in specs]
    out = pl.pallas_call(
        functools.partial(_attn_kernel, n_cast=len(flat)),
        grid=(batch, N_KV_HEADS, nq, nk),
        in_specs=[pl.BlockSpec((tq, gw), lambda b, h, qi, ki: (b * nq + qi, h)),
                  pl.BlockSpec((tk, HEAD_DIM), lambda b, h, qi, ki: (b * nk + ki, k_col0 + h)),
                  pl.BlockSpec((HEAD_DIM, tk), lambda b, h, qi, ki: (b * N_KV_HEADS + h, ki))] + slabs,
        out_specs=[pl.BlockSpec((tq, gw), lambda b, h, qi, ki: (b * nq + qi, h))] + slabs,
        out_shape=[jax.ShapeDtypeStruct((m, N_Q_HEADS * HEAD_DIM), BF16)]
        + [jax.ShapeDtypeStruct(f.shape, BF16) for f in flat],
        scratch_shapes=[pltpu.VMEM((8, tq), F32), pltpu.VMEM((8, tq), F32),
                        pltpu.VMEM((Q_PER_KV, HEAD_DIM, tq), F32)],
        compiler_params=_params("parallel", "parallel", "parallel", "arbitrary"),
        name="gqa_attention",
    )(qkv, qkv, vt, *flat)
    cast = iter(out[1:])
    return out[0], [next(cast) if r else w.astype(BF16) for w, r in zip(weights, rides)]


def _hgrn_kernel(*refs, reverse, final, nc, hb):
    if final:
        q_ref, z_ref, v_ref, lb_ref, tri_ref, prev_ref, g_ref, gain_ref, o_ref, st_ref = refs
    else:
        q_ref, z_ref, v_ref, lb_ref, tri_ref, o_ref, st_ref = refs
    C, D = HG_CHUNK, HG_DIM
    T, W = nc * C, hb * D

    @pl.when(pl.program_id(2) == 0)
    def _():
        st_ref[...] = jnp.zeros_like(st_ref)

    lb = lb_ref[...]
    half_span = 0.5 * (1.0 - lb)
    g = half_span * jnp.tanh(0.5 * z_ref[...])
    key = half_span - g
    logf = jnp.log2(0.5 * (1.0 + lb) + g)

    hi = logf.astype(BF16)
    r1 = logf - hi.astype(F32)
    mid = r1.astype(BF16)
    lo = (r1 - mid.astype(F32)).astype(BF16)
    cat = jnp.concatenate([hi, mid, lo], axis=1)
    tri = tri_ref[...]
    parts = []
    for s in range(T // TRI_BLOCK):
        bc = _dot(tri, cat[s * TRI_BLOCK:(s + 1) * TRI_BLOCK, :])
        parts.append((bc[:, :W] + bc[:, W:2 * W]) + bc[:, 2 * W:])
    b = parts[0] if len(parts) == 1 else jnp.concatenate(parts, axis=0)
    b3 = b.reshape(nc, C, W)
    ref_row = C // 2 if reverse else C // 2 - 1
    end_row = 0 if reverse else C - 1
    bref = b3[:, ref_row:ref_row + 1, :]
    btot = b3[:, end_row:end_row + 1, :]

    q3 = q_ref[...].astype(F32).reshape(nc, C, W)
    q_in = q3 * jnp.exp2(b3 - bref)
    k_in = key.reshape(nc, C, W) * jnp.exp2(bref - b3)
    q_st = (q_in * jnp.exp2(bref)).astype(BF16).reshape(T, W)
    k_up = (k_in * jnp.exp2(btot - bref)).astype(BF16).reshape(T, W)
    dec = jnp.exp2(btot)
    q_in = q_in.astype(BF16).reshape(T, W)
    k_in = k_in.astype(BF16).reshape(T, W)

    P = 2 * C
    rowi = lax.broadcasted_iota(jnp.int32, (P, P), 0)
    coli = lax.broadcasted_iota(jnp.int32, (P, P), 1)
    mask = ((rowi >= C) == (coli >= C)) & ((coli >= rowi) if reverse else (coli <= rowi))
    order = range(nc - 1, -1, -1) if reverse else range(nc)
    for h in range(hb):
        sl = slice(h * D, (h + 1) * D)
        scores = []
        for pr in range(nc // 2):
            rows = slice(pr * P, (pr + 1) * P)
            a = lax.dot_general(q_in[rows, sl], k_in[rows, sl], NT_DIMS, preferred_element_type=F32)
            scores.append(jnp.where(mask, a, 0.0).astype(BF16))
        upd_t = [lax.dot_general(v_ref[c * C:(c + 1) * C, sl], k_up[c * C:(c + 1) * C, sl], TN_DIMS,
                                 preferred_element_type=F32) for c in range(nc)]
        state_t = st_ref[h]
        entering = [None] * nc
        for c in order:
            entering[c] = state_t.astype(BF16)
            state_t = state_t * dec[c, :, sl] + upd_t[c]
        st_ref[h] = state_t
        for pr in range(nc // 2):
            rows = slice(pr * P, (pr + 1) * P)
            o_pair = _dot(scores[pr], v_ref[rows, sl])
            for half in range(2):
                c = 2 * pr + half
                rows_c = slice(c * C, (c + 1) * C)
                o = o_pair[half * C:(half + 1) * C] + lax.dot_general(
                    q_st[rows_c, sl], entering[c], NT_DIMS, preferred_element_type=F32)
                if final:
                    tot = o + prev_ref[rows_c, sl]
                    ms = jnp.mean(tot * tot, axis=-1, keepdims=True)
                    y = tot * lax.rsqrt(ms + RMS_EPS) * gain_ref[:, sl] * g_ref[rows_c, sl].astype(F32)
                    o_ref[rows_c, sl] = y.astype(o_ref.dtype)
                else:
                    o_ref[rows_c, sl] = o


def _hgrn_pass(qh, z, vg, lb, tri, batch, seq, *, reverse, prev=None, gain=None, nc, hb):
    m, hv = qh.shape
    T, W = nc * HG_CHUNK, hb * HG_DIM
    nt, nh = seq // T, hv // W
    final = prev is not None

    def tmap(b, h, n):
        return b * nt + ((nt - 1 - n) if reverse else n)

    zoff = nh if reverse else 0
    in_specs = [pl.BlockSpec((T, W), lambda b, h, n: (tmap(b, h, n), h)),
                pl.BlockSpec((T, W), lambda b, h, n: (tmap(b, h, n), h + zoff)),
                pl.BlockSpec((T, W), lambda b, h, n: (tmap(b, h, n), h)),
                pl.BlockSpec((1, W), lambda b, h, n: (0, h)),
                pl.BlockSpec((TRI_BLOCK, TRI_BLOCK), lambda b, h, n: (0, 0))]
    args = [qh, z, vg, lb, tri]
    if final:
        in_specs += [pl.BlockSpec((T, W), lambda b, h, n: (tmap(b, h, n), h)),
                     pl.BlockSpec((T, W), lambda b, h, n: (tmap(b, h, n), h + nh)),
                     pl.BlockSpec((1, W), lambda b, h, n: (0, h))]
        args += [prev, vg, gain]
    return pl.pallas_call(
        functools.partial(_hgrn_kernel, reverse=reverse, final=final, nc=nc, hb=hb),
        grid=(batch, nh, nt),
        in_specs=in_specs,
        out_specs=pl.BlockSpec((T, W), lambda b, h, n: (tmap(b, h, n), h)),
        out_shape=jax.ShapeDtypeStruct((m, hv), BF16 if final else F32),
        scratch_shapes=[pltpu.VMEM((hb, HG_DIM, HG_DIM), F32)],
        compiler_params=_params("parallel", "parallel", "arbitrary"),
        name="hgrn2_bwd_merge" if final else "hgrn2_fwd",
    )(*args)


def _chunk_triangle(reverse):
    i = jnp.arange(TRI_BLOCK)
    same = (i[:, None] // HG_CHUNK) == (i[None, :] // HG_CHUNK)
    tri = (i[None, :] >= i[:, None]) if reverse else (i[None, :] <= i[:, None])
    return (same & tri).astype(BF16)


def _mix_kernel(x_ref, ya_ref, yh_ref, wga_ref, wgh_ref, wpa_ref, wpb_ref, bga_ref, bgh_ref, o_ref):
    x = x_ref[...]
    ga = _sigmoid(_dot(x, wga_ref[...]) + bga_ref[...])
    gh = _sigmoid(_dot(x, wgh_ref[...]) + bgh_ref[...])
    o = ga * _dot(ya_ref[...], wpa_ref[...]) + gh * _dot(yh_ref[...], wpb_ref[...])
    o_ref[...] = o.astype(o_ref.dtype)


def _mix(xb, ya, yh, w_gate_b, b_gate, w_pa_b, w_pb_b, *, tm, tn):
    m, d = xb.shape
    da, dh = ya.shape[1], yh.shape[1]
    nj = d // tn
    return pl.pallas_call(
        _mix_kernel,
        grid=(m // tm, nj),
        in_specs=[pl.BlockSpec((tm, d), lambda i, j: (i, 0)),
                  pl.BlockSpec((tm, da), lambda i, j: (i, 0)),
                  pl.BlockSpec((tm, dh), lambda i, j: (i, 0)),
                  pl.BlockSpec((d, tn), lambda i, j: (0, j)),
                  pl.BlockSpec((d, tn), lambda i, j: (0, j + nj)),
                  pl.BlockSpec((da, tn), lambda i, j: (0, j)),
                  pl.BlockSpec((dh, tn), lambda i, j: (0, j)),
                  pl.BlockSpec((1, tn), lambda i, j: (0, j)),
                  pl.BlockSpec((1, tn), lambda i, j: (0, j + nj))],
        out_specs=pl.BlockSpec((tm, tn), lambda i, j: (i, j)),
        out_shape=jax.ShapeDtypeStruct((m, d), BF16),
        compiler_params=_params("parallel", "arbitrary"),
        name="gated_branch_merge",
    )(xb, ya, yh, w_gate_b, w_gate_b, w_pa_b, w_pb_b, b_gate, b_gate)


def _mm_res_kernel(a_ref, w_ref, res_ref, o_ref, *, alpha):
    o_ref[...] = alpha * res_ref[...] + _dot(a_ref[...], w_ref[...])


def _mm_res(a, wb, res3, alpha, *, tm, name):
    m, k = a.shape
    nj, _, tn = res3.shape
    return pl.pallas_call(
        functools.partial(_mm_res_kernel, alpha=alpha),
        grid=(m // tm, nj),
        in_specs=[pl.BlockSpec((tm, k), lambda i, j: (i, 0)),
                  pl.BlockSpec((k, tn), lambda i, j: (0, j)),
                  pl.BlockSpec((None, tm, tn), lambda i, j: (j, i, 0))],
        out_specs=pl.BlockSpec((tm, tn), lambda i, j: (i, j)),
        out_shape=jax.ShapeDtypeStruct((m, nj * tn), F32),
        compiler_params=_params("parallel", "arbitrary"),
        name=name,
    )(a, wb, res3)


def _layer_norm_blocks(load, store, g_ref, b_ref, rows, groups, rb):
    width = groups * LANES
    for r in range(rows // rb):
        rs = slice(r * rb, (r + 1) * rb)

        def row_sum(fn):
            acc = fn(load(rs, 0))
            for k in range(1, groups):
                acc = acc + fn(load(rs, k))
            return jnp.sum(acc, axis=-1, keepdims=True)

        mu = row_sum(lambda t: t) / width
        inv = lax.rsqrt(row_sum(lambda t: (t - mu) * (t - mu)) / width + LN_EPS)
        for k in range(groups):
            cols = slice(k * LANES, (k + 1) * LANES)
            store(rs, k, (load(rs, k) - mu) * inv * g_ref[:, cols] + b_ref[:, cols])


def _mm_res_ln_kernel(a_ref, w_ref, res_ref, g_ref, b_ref, o_ref, ob_ref, *, alpha):
    j = pl.program_id(1)
    nj, _, tn = o_ref.shape
    o_ref[j] = alpha * res_ref[...] + _dot(a_ref[...], w_ref[...])

    @pl.when(j == nj - 1)
    def _():
        per = tn // LANES
        at = lambda k: slice((k % per) * LANES, (k % per + 1) * LANES)

        def store(rs, k, y):
            o_ref[k // per, rs, at(k)] = y
            ob_ref[rs, k * LANES:(k + 1) * LANES] = y.astype(ob_ref.dtype)

        _layer_norm_blocks(lambda rs, k: o_ref[k // per, rs, at(k)], store, g_ref, b_ref,
                           o_ref.shape[1], nj * per, LN_ROW_BLOCK)


def _mm_res_ln(a, wb, res, alpha, g, b, *, tm, tn, name):
    m, k = a.shape
    n = wb.shape[1]
    nj = n // tn
    vec = pl.BlockSpec((1, n), lambda i, j: (0, 0))
    return pl.pallas_call(
        functools.partial(_mm_res_ln_kernel, alpha=alpha),
        grid=(m // tm, nj),
        in_specs=[pl.BlockSpec((tm, k), lambda i, j: (i, 0)),
                  pl.BlockSpec((k, tn), lambda i, j: (0, j)),
                  pl.BlockSpec((tm, tn), lambda i, j: (i, j)),
                  vec, vec],
        out_specs=[pl.BlockSpec((nj, tm, tn), lambda i, j: (0, i, 0)),
                   pl.BlockSpec((tm, n), lambda i, j: (i, 0))],
        out_shape=[jax.ShapeDtypeStruct((nj, m, tn), F32), jax.ShapeDtypeStruct((m, n), BF16)],
        compiler_params=_params("parallel", "arbitrary"),
        name=name,
    )(a, wb, res, g, b)


def _ffn_up_kernel(x_ref, xp_ref, xn_ref, wu_ref, wg_ref, cw_ref, cb_ref, o_ref, xh_ref, g_ref, *, blocks_per_seq):
    i, j = pl.program_id(0), pl.program_id(1)
    tm = x_ref.shape[0]
    halo = xp_ref.shape[0]

    @pl.when(j == 0)
    def _():
        first = (i % blocks_per_seq) == 0
        last = (i % blocks_per_seq) == blocks_per_seq - 1
        xh_ref[0:halo] = jnp.where(first, jnp.zeros_like(xp_ref), xp_ref[...])
        xh_ref[halo:halo + tm] = x_ref[...]
        xh_ref[halo + tm:] = jnp.where(last, jnp.zeros_like(xn_ref), xn_ref[...])

    parts = g_ref.shape[0]
    rows = tm // parts
    w = cw_ref[...]

    def matmuls(s):
        g_ref[s] = _dot(xh_ref[s * rows:(s + 1) * rows + 2 * halo], wg_ref[...])
        return _dot(xh_ref[halo + s * rows:halo + (s + 1) * rows], wu_ref[...])

    def gate(s, u):
        win = g_ref[s, halo - 8:halo + rows + 8]
        g_prev = pltpu.roll(win, 1, 0)[8:8 + rows]
        g_next = pltpu.roll(win, rows + 15, 0)[8:8 + rows]
        gc = g_prev * w[0:1] + win[8:8 + rows] * w[1:2] + g_next * w[2:3] + cb_ref[...]
        o_ref[s * rows:(s + 1) * rows] = (_silu(gc) * u).astype(o_ref.dtype)

    u_prev = matmuls(0)
    for s in range(1, parts):
        u_next = matmuls(s)
        gate(s - 1, u_prev)
        u_prev = u_next
    gate(parts - 1, u_prev)


def _ffn_up(xb, w_up_b, conv_w, conv_b, seq, *, tm, tn, parts):
    m, d = xb.shape
    dff = w_up_b.shape[1] // 2
    nj = dff // tn
    halo = 16
    rb = tm // halo
    last_halo = m // halo - 1
    return pl.pallas_call(
        functools.partial(_ffn_up_kernel, blocks_per_seq=seq // tm),
        grid=(m // tm, nj),
        in_specs=[pl.BlockSpec((tm, d), lambda i, j: (i, 0), pipeline_mode=pl.Buffered(1)),
                  pl.BlockSpec((halo, d), lambda i, j: (jnp.maximum(i * rb - 1, 0), 0)),
                  pl.BlockSpec((halo, d), lambda i, j: (jnp.minimum((i + 1) * rb, last_halo), 0)),
                  pl.BlockSpec((d, tn), lambda i, j: (0, j)),
                  pl.BlockSpec((d, tn), lambda i, j: (0, j + nj)),
                  pl.BlockSpec((3, tn), lambda i, j: (0, j)),
                  pl.BlockSpec((1, tn), lambda i, j: (0, j))],
        out_specs=pl.BlockSpec((tm, tn), lambda i, j: (i, j)),
        out_shape=jax.ShapeDtypeStruct((m, dff), BF16),
        scratch_shapes=[pltpu.VMEM((tm + 2 * halo, d), BF16),
                        pltpu.VMEM((parts, tm // parts + 2 * halo, tn), F32)],
        compiler_params=_params("parallel", "arbitrary"),
        name="ffn_up_conv_gate",
    )(xb, xb, xb, w_up_b, w_up_b, conv_w, conv_b)


def _ln_ple_kernel(z_ref, g_ref, b_ref, pb_ref, wpg_ref, wple_ref, o_ref, x_ref, xb_ref):
    j = pl.program_id(1)
    nj, _, tn = x_ref.shape

    @pl.when(j == 0)
    def _():
        per = tn // LANES

        def store(rs, k, y):
            x_ref[k // per, rs, (k % per) * LANES:(k % per + 1) * LANES] = y
            xb_ref[rs, k * LANES:(k + 1) * LANES] = y.astype(xb_ref.dtype)

        _layer_norm_blocks(lambda rs, k: z_ref[rs, k * LANES:(k + 1) * LANES], store, g_ref, b_ref,
                           z_ref.shape[0], nj * per, LN_ROW_BLOCK)

    gate = _sigmoid(_dot(xb_ref[...], wpg_ref[...]))
    o_ref[...] = x_ref[j] + gate * _dot(pb_ref[...], wple_ref[...])


def _ln_ple(z, g, b, pb, w_pg_b, w_ple_b, *, tm, tn):
    m, d = z.shape
    dp = pb.shape[1]
    vec = pl.BlockSpec((1, d), lambda i, j: (0, 0))
    return pl.pallas_call(
        _ln_ple_kernel,
        grid=(m // tm, d // tn),
        in_specs=[pl.BlockSpec((tm, d), lambda i, j: (i, 0)),
                  vec, vec,
                  pl.BlockSpec((tm, dp), lambda i, j: (i, 0)),
                  pl.BlockSpec((d, tn), lambda i, j: (0, j)),
                  pl.BlockSpec((dp, tn), lambda i, j: (0, j))],
        out_specs=pl.BlockSpec((tm, tn), lambda i, j: (i, j)),
        out_shape=jax.ShapeDtypeStruct((m, d), F32),
        scratch_shapes=[pltpu.VMEM((d // tn, tm, tn), F32), pltpu.VMEM((tm, d), BF16)],
        compiler_params=_params("parallel", "arbitrary"),
        name="ln_ple_gate",
    )(z, g, b, pb, w_pg_b, w_ple_b)


def _rope_tables(seq):
    pos = jnp.arange(seq)
    row = (pos // GRID_W).astype(F32)
    col = (pos % GRID_W).astype(F32)
    sec = HEAD_DIM // 2
    inv = ROPE_THETA ** (-jnp.arange(0, sec, 2, dtype=F32) / sec)
    ang_r = row[:, None] * inv[None, :]
    ang_c = col[:, None] * inv[None, :]
    ang = jnp.concatenate([ang_r, ang_r, ang_c, ang_c], axis=-1)
    cos, sin = jnp.cos(ang), jnp.sin(ang)
    first_half = (jnp.arange(HEAD_DIM) % sec) < sec // 2
    sa = jnp.where(first_half[None, :], -sin, 0.0)
    sb = jnp.where(first_half[None, :], 0.0, sin)
    return cos, sa, sb


def _layer(x, p, w_in, q_norm, k_norm, lb_f, lb_b, hg_norm, w_pa, w_pb, w_gate, b_gate, w_o,
           ln1_g, ln1_b, w_up, conv_w, conv_b, w_down, ln2_g, ln2_b, w_pg, w_ple,
           batch, seq, alpha, tables, tris):
    m, d = x.shape
    attn_dim = N_Q_HEADS * HEAD_DIM
    kv_dim = N_KV_HEADS * HEAD_DIM
    hq = HG_HEADS * HG_DIM
    row2 = lambda v: v.reshape(1, -1).astype(F32)
    bf = lambda v: v.astype(BF16)

    xb = bf(x)
    w_in_b = bf(w_in)
    cos, sa, sb = tables
    tm = min(1024, seq)

    qkv = _qkv_proj(xb, w_in_b, cos, sa, sb, row2(q_norm), row2(k_norm), seq, tm=tm, tn=512)
    c0 = attn_dim + 2 * kv_dim
    qh = _proj(xb, w_in_b, c0, hq, 0, BF16, tm=tm, tn=512, name="hgrn_q_proj")
    z = _proj(xb, w_in_b, c0 + hq, 2 * hq, 2 * hq // 512, F32, tm=tm, tn=512, name="hgrn_gate_proj")
    vg = _proj(xb, w_in_b, c0 + 3 * hq, 2 * hq, hq // 512, BF16, tm=tm, tn=512, name="hgrn_vg_proj")

    v = qkv[:, attn_dim + kv_dim:]
    vt = v.reshape(batch, seq, kv_dim).transpose(0, 2, 1).reshape(batch * kv_dim, seq)
    later_weights = [w_gate, w_pa, w_pb, w_o, w_up, w_down, w_pg, w_ple]
    y_attn, (w_gate_b, w_pa_b, w_pb_b, w_o_b, w_up_b, w_down_b, w_pg_b, w_ple_b) = _attention(
        qkv, vt, later_weights, batch, seq, tq=min(1024, seq), tk=min(2048, seq))

    nc = min(8, seq // HG_CHUNK)
    o_fwd = _hgrn_pass(qh, z, vg, row2(lb_f), tris[0], batch, seq, reverse=False, nc=nc, hb=4)
    y_hgrn = _hgrn_pass(qh, z, vg, row2(lb_b), tris[1], batch, seq, reverse=True,
                        prev=o_fwd, gain=row2(hg_norm), nc=nc, hb=4)

    mixed = _mix(xb, y_attn, y_hgrn, w_gate_b, row2(b_gate), w_pa_b, w_pb_b, tm=min(512, seq), tn=512)
    x1_blocked, x1b = _mm_res_ln(mixed, w_o_b, x, alpha, row2(ln1_g), row2(ln1_b),
                                 tm=min(512, seq), tn=512, name="attn_out_residual_ln")

    act = _ffn_up(x1b, w_up_b, conv_w.astype(F32), row2(conv_b), seq, tm=min(2048, seq), tn=256, parts=4)
    z2 = _mm_res(act, w_down_b, x1_blocked, alpha, tm=min(512, seq), name="ffn_down_residual")

    return _ln_ple(z2, row2(ln2_g), row2(ln2_b), bf(p), w_pg_b, w_ple_b, tm=min(512, seq), tn=512)


def kernel(x, p, w_in, q_norm, k_norm, lb_logits, hg_norm, w_pa, w_pb, w_gate, b_gate, w_o, ln1_g, ln1_b,
           w_up, conv_w, conv_b, w_down, ln2_g, ln2_b, w_pg, w_ple):
    batch, seq, d = x.shape
    depth = w_in.shape[0]
    alpha = (2.0 * depth) ** 0.25
    tables = _rope_tables(seq)
    tris = (_chunk_triangle(False), _chunk_triangle(True))
    lb_all = jnp.cumsum(jax.nn.softmax(lb_logits.astype(F32), axis=1), axis=1)
    h = x.reshape(batch * seq, d)
    for i in range(depth):
        h = _layer(h, p[i].reshape(batch * seq, -1), w_in[i], q_norm[i], k_norm[i], lb_all[0, i], lb_all[1, i],
                   hg_norm[i], w_pa[i], w_pb[i], w_gate[i], b_gate[i], w_o[i], ln1_g[i], ln1_b[i],
                   w_up[i], conv_w[i], conv_b[i], w_down[i], ln2_g[i], ln2_b[i], w_pg[i], w_ple[i],
                   batch, seq, alpha, tables, tris)
    return h.reshape(batch, seq, d)
```

```python
import functools

import jax
import jax.numpy as jnp
from jax import lax
from jax.experimental import pallas as pl
from jax.experimental.pallas import tpu as pltpu

F32 = jnp.float32
BF16 = jnp.bfloat16

GRID_W = 64
HEAD_DIM = 128
N_Q_HEADS = 16
N_KV_HEADS = 4
Q_PER_KV = N_Q_HEADS // N_KV_HEADS
ROPE_THETA = 10000.0
HG_HEADS = 16
HG_DIM = 128
HG_CHUNK = 64
RMS_EPS = 1e-6
LN_EPS = 1e-5

LOG2_E = 1.4426950408889634
LANES = 128
TRI_BLOCK = 256
LN_ROW_BLOCK = 64
VMEM_LIMIT = 56 * 1024 * 1024

NT_DIMS = (((1,), (1,)), ((), ()))
TN_DIMS = (((0,), (0,)), ((), ()))


def _params(*sem):
    return pltpu.CompilerParams(dimension_semantics=sem, vmem_limit_bytes=VMEM_LIMIT)


def _dot(a, b):
    return jnp.dot(a, b, preferred_element_type=F32)


def _sigmoid(v):
    return 0.5 + 0.5 * jnp.tanh(0.5 * v)


def _silu(v):
    h = 0.5 * v
    return h + h * jnp.tanh(h)


def _cast_slabs(weights, steps, step_index):
    tile = 16

    def spec(w):
        ntiles = w.shape[0] // tile
        if w.shape[0] % tile or (ntiles % steps and steps % ntiles):
            return None
        per_step, hold = max(ntiles // steps, 1), max(steps // ntiles, 1)
        return pl.BlockSpec((tile * per_step, w.shape[1]), lambda *ids: (step_index(*ids) // hold, 0))

    specs = [spec(w) for w in weights]
    rides = [s is not None for s in specs]
    return [w for w, r in zip(weights, rides) if r], [s for s in specs if s is not None], rides


def _cast_in_kernel(w_refs, wb_refs):
    for w_ref, wb_ref in zip(w_refs, wb_refs):
        wb_ref[...] = w_ref[...].astype(wb_ref.dtype)


def _cast_results(weights, rides, cast_outputs):
    cast = iter(cast_outputs)
    return [next(cast) if r else w.astype(BF16) for w, r in zip(weights, rides)]


def _qkv_kernel(x_ref, w_ref, c0_ref, c1_ref, c2_ref, o_ref, *, nqb, nkb, parts):
    j = pl.program_id(1)
    rows = x_ref.shape[0] // parts
    quarter = HEAD_DIM // 4

    def norm_rope(rs, acc):
        c0, c1, c2 = c0_ref[rs], c1_ref[rs], c2_ref[rs]
        for h in range(acc.shape[1] // HEAD_DIM):
            sl = slice(h * HEAD_DIM, (h + 1) * HEAD_DIM)
            blk = acc[:, sl]
            inv = lax.rsqrt(jnp.mean(blk * blk, axis=-1, keepdims=True) + RMS_EPS)
            rot = blk * c0 + pltpu.roll(blk, HEAD_DIM - quarter, 1) * c1 + pltpu.roll(blk, quarter, 1) * c2
            o_ref[rs, sl] = (rot * inv).astype(o_ref.dtype)

    def passthrough(rs, acc):
        o_ref[rs] = acc.astype(o_ref.dtype)

    def run(epilogue):
        part = lambda s: slice(s * rows, (s + 1) * rows)
        acc_prev = _dot(x_ref[part(0)], w_ref[...])
        for s in range(1, parts):
            acc = _dot(x_ref[part(s)], w_ref[...])
            epilogue(part(s - 1), acc_prev)
            acc_prev = acc
        epilogue(part(parts - 1), acc_prev)

    pl.when(j < nqb + nkb)(lambda: run(norm_rope))
    pl.when(j >= nqb + nkb)(lambda: run(passthrough))


def _qkv_proj(xb, w_in_b, tables, seq, *, tm, tn, parts):
    m, d = xb.shape
    attn_dim = N_Q_HEADS * HEAD_DIM
    kv_dim = N_KV_HEADS * HEAD_DIM
    n = attn_dim + 2 * kv_dim
    nsb = seq // tm
    nqb = attn_dim // tn
    tab = pl.BlockSpec((None, tm, HEAD_DIM), lambda i, j: ((j >= nqb).astype(jnp.int32), i % nsb, 0))
    return pl.pallas_call(
        functools.partial(_qkv_kernel, nqb=nqb, nkb=kv_dim // tn, parts=parts),
        grid=(m // tm, n // tn),
        in_specs=[pl.BlockSpec((tm, d), lambda i, j: (i, 0)),
                  pl.BlockSpec((d, tn), lambda i, j: (0, j)),
                  tab, tab, tab],
        out_specs=pl.BlockSpec((tm, tn), lambda i, j: (i, j)),
        out_shape=jax.ShapeDtypeStruct((m, n), BF16),
        compiler_params=_params("parallel", "arbitrary"),
        name="qkv_proj",
    )(xb, w_in_b, *tables)


def _proj_kernel(x_ref, w_ref, o_ref, *, silu_from):
    j = pl.program_id(1)
    acc = _dot(x_ref[...], w_ref[...])

    @pl.when(j >= silu_from)
    def _():
        o_ref[...] = _silu(acc).astype(o_ref.dtype)

    @pl.when(j < silu_from)
    def _():
        o_ref[...] = acc.astype(o_ref.dtype)


def _proj(xb, wb, col0, ncols, silu_from, out_dtype, *, tm, tn, name):
    m, d = xb.shape
    off = col0 // tn
    return pl.pallas_call(
        functools.partial(_proj_kernel, silu_from=silu_from),
        grid=(m // tm, ncols // tn),
        in_specs=[pl.BlockSpec((tm, d), lambda i, j: (i, 0)),
                  pl.BlockSpec((d, tn), lambda i, j: (0, j + off))],
        out_specs=pl.BlockSpec((tm, tn), lambda i, j: (i, j)),
        out_shape=jax.ShapeDtypeStruct((m, ncols), out_dtype),
        compiler_params=_params("parallel", "arbitrary"),
        name=name,
    )(xb, wb)


def _attn_kernel(*refs, n_cast):
    q_ref, k_ref, vt_ref = refs[:3]
    w_refs = refs[3:3 + n_cast]
    o_ref = refs[3 + n_cast]
    wb_refs = refs[4 + n_cast:4 + 2 * n_cast]
    m_ref, l_ref, acc_ref = refs[4 + 2 * n_cast:]
    ki = pl.program_id(3)

    _cast_in_kernel(w_refs, wb_refs)

    @pl.when(ki == 0)
    def _():
        m_ref[...] = jnp.full_like(m_ref, -1e30)
        l_ref[...] = jnp.zeros_like(l_ref)
        acc_ref[...] = jnp.zeros_like(acc_ref)

    k = k_ref[...]
    vt = vt_ref[...]

    def scores_t(g):
        return lax.dot_general(k, q_ref[:, g * HEAD_DIM:(g + 1) * HEAD_DIM], NT_DIMS,
                               preferred_element_type=F32)

    st_next = scores_t(0)
    for g in range(Q_PER_KV):
        row = slice(g, g + 1)
        st = st_next
        if g + 1 < Q_PER_KV:
            st_next = scores_t(g + 1)
        m_old = m_ref[row, :]
        m_new = jnp.maximum(m_old, jnp.max(st, axis=0, keepdims=True))
        alpha = jnp.exp2(m_old - m_new)
        p = jnp.exp2(st - m_new)
        l_ref[row, :] = alpha * l_ref[row, :] + jnp.sum(p, axis=0, keepdims=True)
        acc_ref[g] = alpha * acc_ref[g] + _dot(vt, p.astype(BF16))
        m_ref[row, :] = m_new

    @pl.when(ki == pl.num_programs(3) - 1)
    def _():
        for g in range(Q_PER_KV):
            o = acc_ref[g] / l_ref[g:g + 1, :]
            o_ref[:, g * HEAD_DIM:(g + 1) * HEAD_DIM] = o.T.astype(o_ref.dtype)


def _attention(qkv, vt, weights, batch, seq, *, tq, tk):
    m = qkv.shape[0]
    gw = Q_PER_KV * HEAD_DIM
    nq, nk = seq // tq, seq // tk
    k_col0 = N_Q_HEADS
    step = lambda b, h, qi, ki: ((b * N_KV_HEADS + h) * nq + qi) * nk + ki
    flat, slabs, rides = _cast_slabs(weights, batch * N_KV_HEADS * nq * nk, step)
    out = pl.pallas_call(
        functools.partial(_attn_kernel, n_cast=len(flat)),
        grid=(batch, N_KV_HEADS, nq, nk),
        in_specs=[pl.BlockSpec((tq, gw), lambda b, h, qi, ki: (b * nq + qi, h)),
                  pl.BlockSpec((tk, HEAD_DIM), lambda b, h, qi, ki: (b * nk + ki, k_col0 + h)),
                  pl.BlockSpec((HEAD_DIM, tk), lambda b, h, qi, ki: (b * N_KV_HEADS + h, ki))] + slabs,
        out_specs=[pl.BlockSpec((tq, gw), lambda b, h, qi, ki: (b * nq + qi, h))] + slabs,
        out_shape=[jax.ShapeDtypeStruct((m, N_Q_HEADS * HEAD_DIM), BF16)]
        + [jax.ShapeDtypeStruct(f.shape, BF16) for f in flat],
        scratch_shapes=[pltpu.VMEM((8, tq), F32), pltpu.VMEM((8, tq), F32),
                        pltpu.VMEM((Q_PER_KV, HEAD_DIM, tq), F32)],
        compiler_params=_params("parallel", "parallel", "parallel", "arbitrary"),
        name="gqa_attention",
    )(qkv, qkv, vt, *flat)
    return out[0], _cast_results(weights, rides, out[1:])


def _hgrn_kernel(*refs, reverse, final, nc, hb):
    if final:
        q_ref, z_ref, v_ref, lb_ref, tri_ref, prev_ref, g_ref, gain_ref, o_ref, st_ref = refs
    else:
        q_ref, z_ref, v_ref, lb_ref, tri_ref, o_ref, st_ref = refs
    C, D = HG_CHUNK, HG_DIM
    T, W = nc * C, hb * D

    @pl.when(pl.program_id(2) == 0)
    def _():
        st_ref[...] = jnp.zeros_like(st_ref)

    lb = lb_ref[...]
    half_span = 0.5 * (1.0 - lb)
    g = half_span * jnp.tanh(0.5 * z_ref[...])
    key = half_span - g
    logf = jnp.log2(0.5 * (1.0 + lb) + g)

    hi = logf.astype(BF16)
    r1 = logf - hi.astype(F32)
    mid = r1.astype(BF16)
    lo = (r1 - mid.astype(F32)).astype(BF16)
    cat = jnp.concatenate([hi, mid, lo], axis=1)
    tri = tri_ref[...]
    parts = []
    for s in range(T // TRI_BLOCK):
        bc = _dot(tri, cat[s * TRI_BLOCK:(s + 1) * TRI_BLOCK, :])
        parts.append((bc[:, :W] + bc[:, W:2 * W]) + bc[:, 2 * W:])
    b = parts[0] if len(parts) == 1 else jnp.concatenate(parts, axis=0)
    b3 = b.reshape(nc, C, W)
    ref_row = C // 2 if reverse else C // 2 - 1
    end_row = 0 if reverse else C - 1
    bref = b3[:, ref_row:ref_row + 1, :]
    btot = b3[:, end_row:end_row + 1, :]

    q3 = q_ref[...].astype(F32).reshape(nc, C, W)
    q_in = q3 * jnp.exp2(b3 - bref)
    k_in = key.reshape(nc, C, W) * jnp.exp2(bref - b3)
    q_st = (q_in * jnp.exp2(bref)).astype(BF16).reshape(T, W)
    k_up = (k_in * jnp.exp2(btot - bref)).astype(BF16).reshape(T, W)
    dec = jnp.exp2(btot)
    q_in = q_in.astype(BF16).reshape(T, W)
    k_in = k_in.astype(BF16).reshape(T, W)

    P = 2 * C
    rowi = lax.broadcasted_iota(jnp.int32, (P, P), 0)
    coli = lax.broadcasted_iota(jnp.int32, (P, P), 1)
    mask = ((rowi >= C) == (coli >= C)) & ((coli >= rowi) if reverse else (coli <= rowi))
    order = range(nc - 1, -1, -1) if reverse else range(nc)
    for h in range(hb):
        sl = slice(h * D, (h + 1) * D)
        scores = []
        for pr in range(nc // 2):
            rows = slice(pr * P, (pr + 1) * P)
            a = lax.dot_general(q_in[rows, sl], k_in[rows, sl], NT_DIMS, preferred_element_type=F32)
            scores.append(jnp.where(mask, a, 0.0).astype(BF16))
        upd_t = [lax.dot_general(v_ref[c * C:(c + 1) * C, sl], k_up[c * C:(c + 1) * C, sl], TN_DIMS,
                                 preferred_element_type=F32) for c in range(nc)]
        state_t = st_ref[h]
        entering = [None] * nc
        for c in order:
            entering[c] = state_t.astype(BF16)
            state_t = state_t * dec[c, :, sl] + upd_t[c]
        st_ref[h] = state_t
        for pr in range(nc // 2):
            rows = slice(pr * P, (pr + 1) * P)
            o_pair = _dot(scores[pr], v_ref[rows, sl])
            for half in range(2):
                c = 2 * pr + half
                rows_c = slice(c * C, (c + 1) * C)
                o = o_pair[half * C:(half + 1) * C] + lax.dot_general(
                    q_st[rows_c, sl], entering[c], NT_DIMS, preferred_element_type=F32)
                if final:
                    tot = o + prev_ref[rows_c, sl]
                    ms = jnp.mean(tot * tot, axis=-1, keepdims=True)
                    y = tot * lax.rsqrt(ms + RMS_EPS) * gain_ref[:, sl] * g_ref[rows_c, sl].astype(F32)
                    o_ref[rows_c, sl] = y.astype(o_ref.dtype)
                else:
                    o_ref[rows_c, sl] = o


def _hgrn_pass(qh, z, vg, lb, tri, batch, seq, *, reverse, prev=None, gain=None, nc, hb):
    m, hv = qh.shape
    T, W = nc * HG_CHUNK, hb * HG_DIM
    nt, nh = seq // T, hv // W
    final = prev is not None

    def tmap(b, h, n):
        return b * nt + ((nt - 1 - n) if reverse else n)

    zoff = nh if reverse else 0
    in_specs = [pl.BlockSpec((T, W), lambda b, h, n: (tmap(b, h, n), h)),
                pl.BlockSpec((T, W), lambda b, h, n: (tmap(b, h, n), h + zoff)),
                pl.BlockSpec((T, W), lambda b, h, n: (tmap(b, h, n), h)),
                pl.BlockSpec((1, W), lambda b, h, n: (0, h)),
                pl.BlockSpec((TRI_BLOCK, TRI_BLOCK), lambda b, h, n: (0, 0))]
    args = [qh, z, vg, lb, tri]
    if final:
        in_specs += [pl.BlockSpec((T, W), lambda b, h, n: (tmap(b, h, n), h)),
                     pl.BlockSpec((T, W), lambda b, h, n: (tmap(b, h, n), h + nh)),
                     pl.BlockSpec((1, W), lambda b, h, n: (0, h))]
        args += [prev, vg, gain]
    return pl.pallas_call(
        functools.partial(_hgrn_kernel, reverse=reverse, final=final, nc=nc, hb=hb),
        grid=(batch, nh, nt),
        in_specs=in_specs,
        out_specs=pl.BlockSpec((T, W), lambda b, h, n: (tmap(b, h, n), h)),
        out_shape=jax.ShapeDtypeStruct((m, hv), BF16 if final else F32),
        scratch_shapes=[pltpu.VMEM((hb, HG_DIM, HG_DIM), F32)],
        compiler_params=_params("parallel", "parallel", "arbitrary"),
        name="hgrn2_bwd_merge" if final else "hgrn2_fwd",
    )(*args)


def _chunk_triangle(reverse):
    i = jnp.arange(TRI_BLOCK)
    same = (i[:, None] // HG_CHUNK) == (i[None, :] // HG_CHUNK)
    tri = (i[None, :] >= i[:, None]) if reverse else (i[None, :] <= i[:, None])
    return (same & tri).astype(BF16)


def _mix_kernel(x_ref, ya_ref, yh_ref, wga_ref, wgh_ref, wpa_ref, wpb_ref, bga_ref, bgh_ref, o_ref):
    x = x_ref[...]
    ga = _sigmoid(_dot(x, wga_ref[...]) + bga_ref[...])
    gh = _sigmoid(_dot(x, wgh_ref[...]) + bgh_ref[...])
    o = ga * _dot(ya_ref[...], wpa_ref[...]) + gh * _dot(yh_ref[...], wpb_ref[...])
    o_ref[...] = o.astype(o_ref.dtype)


def _mix(xb, ya, yh, w_gate_b, b_gate, w_pa_b, w_pb_b, *, tm, tn):
    m, d = xb.shape
    da, dh = ya.shape[1], yh.shape[1]
    nj = d // tn
    return pl.pallas_call(
        _mix_kernel,
        grid=(m // tm, nj),
        in_specs=[pl.BlockSpec((tm, d), lambda i, j: (i, 0)),
                  pl.BlockSpec((tm, da), lambda i, j: (i, 0)),
                  pl.BlockSpec((tm, dh), lambda i, j: (i, 0)),
                  pl.BlockSpec((d, tn), lambda i, j: (0, j)),
                  pl.BlockSpec((d, tn), lambda i, j: (0, j + nj)),
                  pl.BlockSpec((da, tn), lambda i, j: (0, j)),
                  pl.BlockSpec((dh, tn), lambda i, j: (0, j)),
                  pl.BlockSpec((1, tn), lambda i, j: (0, j)),
                  pl.BlockSpec((1, tn), lambda i, j: (0, j + nj))],
        out_specs=pl.BlockSpec((tm, tn), lambda i, j: (i, j)),
        out_shape=jax.ShapeDtypeStruct((m, d), BF16),
        compiler_params=_params("parallel", "arbitrary"),
        name="gated_branch_merge",
    )(xb, ya, yh, w_gate_b, w_gate_b, w_pa_b, w_pb_b, b_gate, b_gate)


def _mm_res_kernel(a_ref, w_ref, res_ref, o_ref, *, alpha):
    o_ref[...] = alpha * res_ref[...] + _dot(a_ref[...], w_ref[...])


def _mm_res(a, wb, res3, alpha, *, tm, name):
    m, k = a.shape
    nj, _, tn = res3.shape
    return pl.pallas_call(
        functools.partial(_mm_res_kernel, alpha=alpha),
        grid=(m // tm, nj),
        in_specs=[pl.BlockSpec((tm, k), lambda i, j: (i, 0)),
                  pl.BlockSpec((k, tn), lambda i, j: (0, j)),
                  pl.BlockSpec((None, tm, tn), lambda i, j: (j, i, 0))],
        out_specs=pl.BlockSpec((tm, tn), lambda i, j: (i, j)),
        out_shape=jax.ShapeDtypeStruct((m, nj * tn), F32),
        compiler_params=_params("parallel", "arbitrary"),
        name=name,
    )(a, wb, res3)


def _layer_norm_blocks(load, store, g_ref, b_ref, rows, groups, rb):
    width = groups * LANES
    for r in range(rows // rb):
        rs = slice(r * rb, (r + 1) * rb)

        def row_sum(fn):
            acc = fn(load(rs, 0))
            for k in range(1, groups):
                acc = acc + fn(load(rs, k))
            return jnp.sum(acc, axis=-1, keepdims=True)

        mu = row_sum(lambda t: t) / width
        inv = lax.rsqrt(row_sum(lambda t: (t - mu) * (t - mu)) / width + LN_EPS)
        for k in range(groups):
            cols = slice(k * LANES, (k + 1) * LANES)
            store(rs, k, (load(rs, k) - mu) * inv * g_ref[:, cols] + b_ref[:, cols])


def _mm_res_ln_kernel(a_ref, w_ref, res_ref, g_ref, b_ref, o_ref, ob_ref, *, alpha):
    j = pl.program_id(1)
    nj, _, tn = o_ref.shape
    o_ref[j] = alpha * res_ref[...] + _dot(a_ref[...], w_ref[...])

    @pl.when(j == nj - 1)
    def _():
        per = tn // LANES
        at = lambda k: slice((k % per) * LANES, (k % per + 1) * LANES)

        def store(rs, k, y):
            o_ref[k // per, rs, at(k)] = y
            ob_ref[rs, k * LANES:(k + 1) * LANES] = y.astype(ob_ref.dtype)

        _layer_norm_blocks(lambda rs, k: o_ref[k // per, rs, at(k)], store, g_ref, b_ref,
                           o_ref.shape[1], nj * per, LN_ROW_BLOCK)


def _mm_res_ln(a, wb, res, alpha, g, b, *, tm, tn, name):
    m, k = a.shape
    n = wb.shape[1]
    nj = n // tn
    vec = pl.BlockSpec((1, n), lambda i, j: (0, 0))
    return pl.pallas_call(
        functools.partial(_mm_res_ln_kernel, alpha=alpha),
        grid=(m // tm, nj),
        in_specs=[pl.BlockSpec((tm, k), lambda i, j: (i, 0)),
                  pl.BlockSpec((k, tn), lambda i, j: (0, j)),
                  pl.BlockSpec((tm, tn), lambda i, j: (i, j)),
                  vec, vec],
        out_specs=[pl.BlockSpec((nj, tm, tn), lambda i, j: (0, i, 0)),
                   pl.BlockSpec((tm, n), lambda i, j: (i, 0))],
        out_shape=[jax.ShapeDtypeStruct((nj, m, tn), F32), jax.ShapeDtypeStruct((m, n), BF16)],
        compiler_params=_params("parallel", "arbitrary"),
        name=name,
    )(a, wb, res, g, b)


def _ffn_up_kernel(*refs, blocks_per_seq, parts, n_cast):
    x_ref, xp_ref, xn_ref, wu_ref, wg_ref, cw_ref, cb_ref = refs[:7]
    o_ref = refs[7 + n_cast]
    xh_ref, g_ref = refs[8 + 2 * n_cast:]
    _cast_in_kernel(refs[7:7 + n_cast], refs[8 + n_cast:8 + 2 * n_cast])
    i, j = pl.program_id(0), pl.program_id(1)
    tm = x_ref.shape[0]
    halo = xp_ref.shape[0]

    @pl.when(j == 0)
    def _():
        first = (i % blocks_per_seq) == 0
        last = (i % blocks_per_seq) == blocks_per_seq - 1
        xh_ref[0:halo] = jnp.where(first, jnp.zeros_like(xp_ref), xp_ref[...])
        xh_ref[halo:halo + tm] = x_ref[...]
        xh_ref[halo + tm:] = jnp.where(last, jnp.zeros_like(xn_ref), xn_ref[...])

    rows = tm // parts
    w = cw_ref[...]

    def matmuls(s):
        lo = 0 if s == 0 else halo + s * rows
        hi = tm + 2 * halo if s == parts - 1 else halo + (s + 1) * rows
        g_ref[lo:hi] = _dot(xh_ref[lo:hi], wg_ref[...])
        return _dot(xh_ref[halo + s * rows:halo + (s + 1) * rows], wu_ref[...])

    def gate(s, u):
        win = g_ref[halo + s * rows - 8:halo + (s + 1) * rows + 8]
        g_prev = pltpu.roll(win, 1, 0)[8:8 + rows]
        g_next = pltpu.roll(win, rows + 15, 0)[8:8 + rows]
        gc = g_prev * w[0:1] + win[8:8 + rows] * w[1:2] + g_next * w[2:3] + cb_ref[...]
        o_ref[s * rows:(s + 1) * rows] = (_silu(gc) * u).astype(o_ref.dtype)

    u_prev = matmuls(0)
    for s in range(1, parts):
        u_next = matmuls(s)
        gate(s - 1, u_prev)
        u_prev = u_next
    gate(parts - 1, u_prev)


def _ffn_up(xb, w_up_b, conv_w, conv_b, weights, seq, *, tm, tn, parts):
    m, d = xb.shape
    dff = w_up_b.shape[1] // 2
    nj = dff // tn
    halo = 16
    rb = tm // halo
    last_halo = m // halo - 1
    flat, slabs, rides = _cast_slabs(weights, (m // tm) * nj, lambda i, j: i * nj + j)
    out = pl.pallas_call(
        functools.partial(_ffn_up_kernel, blocks_per_seq=seq // tm, parts=parts, n_cast=len(flat)),
        grid=(m // tm, nj),
        in_specs=[pl.BlockSpec((tm, d), lambda i, j: (i, 0), pipeline_mode=pl.Buffered(1)),
                  pl.BlockSpec((halo, d), lambda i, j: (jnp.maximum(i * rb - 1, 0), 0)),
                  pl.BlockSpec((halo, d), lambda i, j: (jnp.minimum((i + 1) * rb, last_halo), 0)),
                  pl.BlockSpec((d, tn), lambda i, j: (0, j)),
                  pl.BlockSpec((d, tn), lambda i, j: (0, j + nj)),
                  pl.BlockSpec((3, tn), lambda i, j: (0, j)),
                  pl.BlockSpec((1, tn), lambda i, j: (0, j))] + slabs,
        out_specs=[pl.BlockSpec((tm, tn), lambda i, j: (i, j))] + slabs,
        out_shape=[jax.ShapeDtypeStruct((m, dff), BF16)] + [jax.ShapeDtypeStruct(f.shape, BF16) for f in flat],
        scratch_shapes=[pltpu.VMEM((tm + 2 * halo, d), BF16),
                        pltpu.VMEM((tm + 2 * halo, tn), F32)],
        compiler_params=_params("parallel", "arbitrary"),
        name="ffn_up_conv_gate",
    )(xb, xb, xb, w_up_b, w_up_b, conv_w, conv_b, *flat)
    return out[0], _cast_results(weights, rides, out[1:])


def _ln_ple_kernel(z_ref, g_ref, b_ref, pb_ref, wpg_ref, wple_ref, o_ref, x_ref, xb_ref):
    j = pl.program_id(1)
    nj, _, tn = x_ref.shape

    @pl.when(j == 0)
    def _():
        per = tn // LANES

        def store(rs, k, y):
            x_ref[k // per, rs, (k % per) * LANES:(k % per + 1) * LANES] = y
            xb_ref[rs, k * LANES:(k + 1) * LANES] = y.astype(xb_ref.dtype)

        _layer_norm_blocks(lambda rs, k: z_ref[rs, k * LANES:(k + 1) * LANES], store, g_ref, b_ref,
                           z_ref.shape[0], nj * per, LN_ROW_BLOCK)

    gate = _sigmoid(_dot(xb_ref[...], wpg_ref[...]))
    o_ref[...] = x_ref[j] + gate * _dot(pb_ref[...], wple_ref[...])


def _ln_ple(z, g, b, pb, w_pg_b, w_ple_b, *, tm, tn):
    m, d = z.shape
    dp = pb.shape[1]
    vec = pl.BlockSpec((1, d), lambda i, j: (0, 0))
    return pl.pallas_call(
        _ln_ple_kernel,
        grid=(m // tm, d // tn),
        in_specs=[pl.BlockSpec((tm, d), lambda i, j: (i, 0)),
                  vec, vec,
                  pl.BlockSpec((tm, dp), lambda i, j: (i, 0)),
                  pl.BlockSpec((d, tn), lambda i, j: (0, j)),
                  pl.BlockSpec((dp, tn), lambda i, j: (0, j))],
        out_specs=pl.BlockSpec((tm, tn), lambda i, j: (i, j)),
        out_shape=jax.ShapeDtypeStruct((m, d), F32),
        scratch_shapes=[pltpu.VMEM((d // tn, tm, tn), F32), pltpu.VMEM((tm, d), BF16)],
        compiler_params=_params("parallel", "arbitrary"),
        name="ln_ple_gate",
    )(z, g, b, pb, w_pg_b, w_ple_b)


def _rope_tables(seq):
    pos = jnp.arange(seq)
    row = (pos // GRID_W).astype(F32)
    col = (pos % GRID_W).astype(F32)
    sec = HEAD_DIM // 2
    inv = ROPE_THETA ** (-jnp.arange(0, sec, 2, dtype=F32) / sec)
    ang_r = row[:, None] * inv[None, :]
    ang_c = col[:, None] * inv[None, :]
    ang = jnp.concatenate([ang_r, ang_r, ang_c, ang_c], axis=-1)
    cos, sin = jnp.cos(ang), jnp.sin(ang)
    first_half = (jnp.arange(HEAD_DIM) % sec) < sec // 2
    sa = jnp.where(first_half[None, :], -sin, 0.0)
    sb = jnp.where(first_half[None, :], 0.0, sin)
    return cos, sa, sb


def _qk_rotary_tables(rope, q_gain, k_gain):
    cos, sa, sb = rope
    quarter = HEAD_DIM // 4

    def fold(gain, scale):
        g = gain.astype(F32) * scale
        return (cos * g[None, :], sa * jnp.roll(g, HEAD_DIM - quarter)[None, :], sb * jnp.roll(g, quarter)[None, :])

    tq = fold(q_gain, HEAD_DIM ** -0.5 * LOG2_E)
    tk = fold(k_gain, 1.0)
    return tuple(jnp.stack([a, b]) for a, b in zip(tq, tk))


def _layer(x, p, w_in, q_norm, k_norm, lb_f, lb_b, hg_norm, w_pa, w_pb, w_gate, b_gate, w_o,
           ln1_g, ln1_b, w_up, conv_w, conv_b, w_down, ln2_g, ln2_b, w_pg, w_ple,
           batch, seq, alpha, tables, tris):
    m, d = x.shape
    attn_dim = N_Q_HEADS * HEAD_DIM
    kv_dim = N_KV_HEADS * HEAD_DIM
    hq = HG_HEADS * HG_DIM
    row2 = lambda v: v.reshape(1, -1).astype(F32)
    bf = lambda v: v.astype(BF16)

    xb = bf(x)
    w_in_b = bf(w_in)
    tm = min(1024, seq)

    qkv = _qkv_proj(xb, w_in_b, _qk_rotary_tables(tables, q_norm, k_norm), seq, tm=tm, tn=512, parts=2)
    c0 = attn_dim + 2 * kv_dim
    qh = _proj(xb, w_in_b, c0, hq, 0, BF16, tm=tm, tn=512, name="hgrn_q_proj")
    z = _proj(xb, w_in_b, c0 + hq, 2 * hq, 2 * hq // 512, F32, tm=tm, tn=512, name="hgrn_gate_proj")
    vg = _proj(xb, w_in_b, c0 + 3 * hq, 2 * hq, hq // 512, BF16, tm=tm, tn=512, name="hgrn_vg_proj")

    v = qkv[:, attn_dim + kv_dim:]
    vt = v.reshape(batch, seq, kv_dim).transpose(0, 2, 1).reshape(batch * kv_dim, seq)
    later_weights = [w_gate, w_pa, w_pb, w_o, w_up, w_pg, w_ple]
    y_attn, (w_gate_b, w_pa_b, w_pb_b, w_o_b, w_up_b, w_pg_b, w_ple_b) = _attention(
        qkv, vt, later_weights, batch, seq, tq=min(1024, seq), tk=min(2048, seq))

    nc = min(8, seq // HG_CHUNK)
    o_fwd = _hgrn_pass(qh, z, vg, row2(lb_f), tris[0], batch, seq, reverse=False, nc=nc, hb=4)
    y_hgrn = _hgrn_pass(qh, z, vg, row2(lb_b), tris[1], batch, seq, reverse=True,
                        prev=o_fwd, gain=row2(hg_norm), nc=nc, hb=4)

    mixed = _mix(xb, y_attn, y_hgrn, w_gate_b, row2(b_gate), w_pa_b, w_pb_b, tm=min(512, seq), tn=512)
    x1_blocked, x1b = _mm_res_ln(mixed, w_o_b, x, alpha, row2(ln1_g), row2(ln1_b),
                                 tm=min(512, seq), tn=512, name="attn_out_residual_ln")

    act, (w_down_b,) = _ffn_up(x1b, w_up_b, conv_w.astype(F32), row2(conv_b), [w_down], seq,
                               tm=min(2048, seq), tn=256, parts=4)
    z2 = _mm_res(act, w_down_b, x1_blocked, alpha, tm=min(512, seq), name="ffn_down_residual")

    return _ln_ple(z2, row2(ln2_g), row2(ln2_b), bf(p), w_pg_b, w_ple_b, tm=min(512, seq), tn=512)


def kernel(x, p, w_in, q_norm, k_norm, lb_logits, hg_norm, w_pa, w_pb, w_gate, b_gate, w_o, ln1_g, ln1_b,
           w_up, conv_w, conv_b, w_down, ln2_g, ln2_b, w_pg, w_ple):
    batch, seq, d = x.shape
    depth = w_in.shape[0]
    alpha = (2.0 * depth) ** 0.25
    tables = _rope_tables(seq)
    tris = (_chunk_triangle(False), _chunk_triangle(True))
    lb_all = jnp.cumsum(jax.nn.softmax(lb_logits.astype(F32), axis=1), axis=1)
    h = x.reshape(batch * seq, d)
    for i in range(depth):
        h = _layer(h, p[i].reshape(batch * seq, -1), w_in[i], q_norm[i], k_norm[i], lb_all[0, i], lb_all[1, i],
                   hg_norm[i], w_pa[i], w_pb[i], w_gate[i], b_gate[i], w_o[i], ln1_g[i], ln1_b[i],
                   w_up[i], conv_w[i], conv_b[i], w_down[i], ln2_g[i], ln2_b[i], w_pg[i], w_ple[i],
                   batch, seq, alpha, tables, tris)
    return h.reshape(batch, seq, d)
```

```python
import functools

import jax
import jax.numpy as jnp
from jax import lax
from jax.experimental import pallas as pl
from jax.experimental.pallas import tpu as pltpu

F32 = jnp.float32
BF16 = jnp.bfloat16

GRID_W = 64
HEAD_DIM = 128
N_Q_HEADS = 16
N_KV_HEADS = 4
Q_PER_KV = N_Q_HEADS // N_KV_HEADS
ROPE_THETA = 10000.0
HG_HEADS = 16
HG_DIM = 128
HG_CHUNK = 64
RMS_EPS = 1e-6
LN_EPS = 1e-5

LOG2_E = 1.4426950408889634
LANES = 128
TRI_BLOCK = 256
LN_ROW_BLOCK = 64
VMEM_LIMIT = 56 * 1024 * 1024

NT_DIMS = (((1,), (1,)), ((), ()))
TN_DIMS = (((0,), (0,)), ((), ()))


def _params(*sem):
    return pltpu.CompilerParams(dimension_semantics=sem, vmem_limit_bytes=VMEM_LIMIT)


def _dot(a, b):
    return jnp.dot(a, b, preferred_element_type=F32)


def _sigmoid(v):
    return 0.5 + 0.5 * jnp.tanh(0.5 * v)


def _silu(v):
    h = 0.5 * v
    return h + h * jnp.tanh(h)


def _cast_slabs(weights, steps, step_index):
    tile = 16

    def spec(w):
        ntiles = w.shape[0] // tile
        if w.shape[0] % tile or (ntiles % steps and steps % ntiles):
            return None
        per_step, hold = max(ntiles // steps, 1), max(steps // ntiles, 1)
        return pl.BlockSpec((tile * per_step, w.shape[1]), lambda *ids: (step_index(*ids) // hold, 0))

    specs = [spec(w) for w in weights]
    rides = [s is not None for s in specs]
    return [w for w, r in zip(weights, rides) if r], [s for s in specs if s is not None], rides


def _cast_in_kernel(w_refs, wb_refs):
    for w_ref, wb_ref in zip(w_refs, wb_refs):
        wb_ref[...] = w_ref[...].astype(wb_ref.dtype)


def _cast_results(weights, rides, cast_outputs):
    cast = iter(cast_outputs)
    return [next(cast) if r else w.astype(BF16) for w, r in zip(weights, rides)]


def _matmul_row_parts(x_ref, w_ref, parts, epilogue):
    rows = x_ref.shape[0] // parts
    part = lambda s: slice(s * rows, (s + 1) * rows)
    acc_prev = _dot(x_ref[part(0)], w_ref[...])
    for s in range(1, parts):
        acc = _dot(x_ref[part(s)], w_ref[...])
        epilogue(part(s - 1), acc_prev)
        acc_prev = acc
    epilogue(part(parts - 1), acc_prev)


def _qkv_kernel(x_ref, w_ref, c0_ref, c1_ref, c2_ref, o_ref, *, nqb, nkb, parts):
    j = pl.program_id(1)
    quarter = HEAD_DIM // 4

    def norm_rope(rs, acc):
        c0, c1, c2 = c0_ref[rs], c1_ref[rs], c2_ref[rs]
        for h in range(acc.shape[1] // HEAD_DIM):
            sl = slice(h * HEAD_DIM, (h + 1) * HEAD_DIM)
            blk = acc[:, sl]
            inv = lax.rsqrt(jnp.mean(blk * blk, axis=-1, keepdims=True) + RMS_EPS)
            rot = blk * c0 + pltpu.roll(blk, HEAD_DIM - quarter, 1) * c1 + pltpu.roll(blk, quarter, 1) * c2
            o_ref[rs, sl] = (rot * inv).astype(o_ref.dtype)

    def passthrough(rs, acc):
        o_ref[rs] = acc.astype(o_ref.dtype)

    pl.when(j < nqb + nkb)(lambda: _matmul_row_parts(x_ref, w_ref, parts, norm_rope))
    pl.when(j >= nqb + nkb)(lambda: _matmul_row_parts(x_ref, w_ref, parts, passthrough))


def _qkv_proj(xb, w_in_b, tables, seq, *, tm, tn, parts):
    m, d = xb.shape
    attn_dim = N_Q_HEADS * HEAD_DIM
    kv_dim = N_KV_HEADS * HEAD_DIM
    n = attn_dim + 2 * kv_dim
    nsb = seq // tm
    nqb = attn_dim // tn
    tab = pl.BlockSpec((None, tm, HEAD_DIM), lambda i, j: ((j >= nqb).astype(jnp.int32), i % nsb, 0))
    return pl.pallas_call(
        functools.partial(_qkv_kernel, nqb=nqb, nkb=kv_dim // tn, parts=parts),
        grid=(m // tm, n // tn),
        in_specs=[pl.BlockSpec((tm, d), lambda i, j: (i, 0)),
                  pl.BlockSpec((d, tn), lambda i, j: (0, j)),
                  tab, tab, tab],
        out_specs=pl.BlockSpec((tm, tn), lambda i, j: (i, j)),
        out_shape=jax.ShapeDtypeStruct((m, n), BF16),
        compiler_params=_params("parallel", "arbitrary"),
        name="qkv_proj",
    )(xb, w_in_b, *tables)


def _proj_kernel(x_ref, w_ref, o_ref, *, silu_from, parts):
    j = pl.program_id(1)

    def store(fn):
        def epilogue(rs, acc):
            o_ref[rs] = fn(acc).astype(o_ref.dtype)
        return epilogue

    pl.when(j >= silu_from)(lambda: _matmul_row_parts(x_ref, w_ref, parts, store(_silu)))
    pl.when(j < silu_from)(lambda: _matmul_row_parts(x_ref, w_ref, parts, store(lambda acc: acc)))


def _proj(xb, wb, col0, ncols, silu_from, out_dtype, *, tm, tn, parts, name):
    m, d = xb.shape
    off = col0 // tn
    return pl.pallas_call(
        functools.partial(_proj_kernel, silu_from=silu_from, parts=parts),
        grid=(m // tm, ncols // tn),
        in_specs=[pl.BlockSpec((tm, d), lambda i, j: (i, 0)),
                  pl.BlockSpec((d, tn), lambda i, j: (0, j + off))],
        out_specs=pl.BlockSpec((tm, tn), lambda i, j: (i, j)),
        out_shape=jax.ShapeDtypeStruct((m, ncols), out_dtype),
        compiler_params=_params("parallel", "arbitrary"),
        name=name,
    )(xb, wb)


def _attn_kernel(*refs, n_cast):
    q_ref, k_ref, vt_ref = refs[:3]
    w_refs = refs[3:3 + n_cast]
    o_ref = refs[3 + n_cast]
    wb_refs = refs[4 + n_cast:4 + 2 * n_cast]
    m_ref, l_ref, acc_ref = refs[4 + 2 * n_cast:]
    ki = pl.program_id(3)

    _cast_in_kernel(w_refs, wb_refs)

    @pl.when(ki == 0)
    def _():
        m_ref[...] = jnp.full_like(m_ref, -1e30)
        l_ref[...] = jnp.zeros_like(l_ref)
        acc_ref[...] = jnp.zeros_like(acc_ref)

    k = k_ref[...]
    vt = vt_ref[...]

    def scores_t(g):
        return lax.dot_general(k, q_ref[:, g * HEAD_DIM:(g + 1) * HEAD_DIM], NT_DIMS,
                               preferred_element_type=F32)

    st_next = scores_t(0)
    for g in range(Q_PER_KV):
        row = slice(g, g + 1)
        st = st_next
        if g + 1 < Q_PER_KV:
            st_next = scores_t(g + 1)
        m_old = m_ref[row, :]
        m_new = jnp.maximum(m_old, jnp.max(st, axis=0, keepdims=True))
        alpha = jnp.exp2(m_old - m_new)
        p = jnp.exp2(st - m_new)
        l_ref[row, :] = alpha * l_ref[row, :] + jnp.sum(p, axis=0, keepdims=True)
        acc_ref[g] = alpha * acc_ref[g] + _dot(vt, p.astype(BF16))
        m_ref[row, :] = m_new

    @pl.when(ki == pl.num_programs(3) - 1)
    def _():
        for g in range(Q_PER_KV):
            o = acc_ref[g] / l_ref[g:g + 1, :]
            o_ref[:, g * HEAD_DIM:(g + 1) * HEAD_DIM] = o.T.astype(o_ref.dtype)


def _attention(qkv, vt, weights, batch, seq, *, tq, tk):
    m = qkv.shape[0]
    gw = Q_PER_KV * HEAD_DIM
    nq, nk = seq // tq, seq // tk
    k_col0 = N_Q_HEADS
    step = lambda b, h, qi, ki: ((b * N_KV_HEADS + h) * nq + qi) * nk + ki
    flat, slabs, rides = _cast_slabs(weights, batch * N_KV_HEADS * nq * nk, step)
    out = pl.pallas_call(
        functools.partial(_attn_kernel, n_cast=len(flat)),
        grid=(batch, N_KV_HEADS, nq, nk),
        in_specs=[pl.BlockSpec((tq, gw), lambda b, h, qi, ki: (b * nq + qi, h)),
                  pl.BlockSpec((tk, HEAD_DIM), lambda b, h, qi, ki: (b * nk + ki, k_col0 + h)),
                  pl.BlockSpec((HEAD_DIM, tk), lambda b, h, qi, ki: (b * N_KV_HEADS + h, ki))] + slabs,
        out_specs=[pl.BlockSpec((tq, gw), lambda b, h, qi, ki: (b * nq + qi, h))] + slabs,
        out_shape=[jax.ShapeDtypeStruct((m, N_Q_HEADS * HEAD_DIM), BF16)]
        + [jax.ShapeDtypeStruct(f.shape, BF16) for f in flat],
        scratch_shapes=[pltpu.VMEM((8, tq), F32), pltpu.VMEM((8, tq), F32),
                        pltpu.VMEM((Q_PER_KV, HEAD_DIM, tq), F32)],
        compiler_params=_params("parallel", "parallel", "parallel", "arbitrary"),
        name="gqa_attention",
    )(qkv, qkv, vt, *flat)
    return out[0], _cast_results(weights, rides, out[1:])


def _hgrn_kernel(*refs, reverse, final, nc, hb):
    if final:
        q_ref, z_ref, v_ref, lb_ref, tri_ref, prev_ref, g_ref, gain_ref, o_ref, st_ref = refs
    else:
        q_ref, z_ref, v_ref, lb_ref, tri_ref, o_ref, st_ref = refs
    C, D = HG_CHUNK, HG_DIM
    T, W = nc * C, hb * D

    @pl.when(pl.program_id(2) == 0)
    def _():
        st_ref[...] = jnp.zeros_like(st_ref)

    lb = lb_ref[...]
    half_span = 0.5 * (1.0 - lb)
    g = half_span * jnp.tanh(0.5 * z_ref[...])
    key = half_span - g
    logf = jnp.log2(0.5 * (1.0 + lb) + g)

    hi = logf.astype(BF16)
    r1 = logf - hi.astype(F32)
    mid = r1.astype(BF16)
    lo = (r1 - mid.astype(F32)).astype(BF16)
    cat = jnp.concatenate([hi, mid, lo], axis=1)
    tri = tri_ref[...]
    parts = []
    for s in range(T // TRI_BLOCK):
        bc = _dot(tri, cat[s * TRI_BLOCK:(s + 1) * TRI_BLOCK, :])
        parts.append((bc[:, :W] + bc[:, W:2 * W]) + bc[:, 2 * W:])
    b = parts[0] if len(parts) == 1 else jnp.concatenate(parts, axis=0)
    b3 = b.reshape(nc, C, W)
    ref_row = C // 2 if reverse else C // 2 - 1
    end_row = 0 if reverse else C - 1
    bref = b3[:, ref_row:ref_row + 1, :]
    btot = b3[:, end_row:end_row + 1, :]

    q3 = q_ref[...].astype(F32).reshape(nc, C, W)
    q_in = q3 * jnp.exp2(b3 - bref)
    k_in = key.reshape(nc, C, W) * jnp.exp2(bref - b3)
    q_st = (q_in * jnp.exp2(bref)).astype(BF16).reshape(T, W)
    k_up = (k_in * jnp.exp2(btot - bref)).astype(BF16).reshape(T, W)
    dec = jnp.exp2(btot)
    q_in = q_in.astype(BF16).reshape(T, W)
    k_in = k_in.astype(BF16).reshape(T, W)

    P = 2 * C
    rowi = lax.broadcasted_iota(jnp.int32, (P, P), 0)
    coli = lax.broadcasted_iota(jnp.int32, (P, P), 1)
    mask = ((rowi >= C) == (coli >= C)) & ((coli >= rowi) if reverse else (coli <= rowi))
    order = range(nc - 1, -1, -1) if reverse else range(nc)
    for h in range(hb):
        sl = slice(h * D, (h + 1) * D)
        scores = []
        for pr in range(nc // 2):
            rows = slice(pr * P, (pr + 1) * P)
            a = lax.dot_general(q_in[rows, sl], k_in[rows, sl], NT_DIMS, preferred_element_type=F32)
            scores.append(jnp.where(mask, a, 0.0).astype(BF16))
        upd_t = [lax.dot_general(v_ref[c * C:(c + 1) * C, sl], k_up[c * C:(c + 1) * C, sl], TN_DIMS,
                                 preferred_element_type=F32) for c in range(nc)]
        state_t = st_ref[h]
        entering = [None] * nc
        for c in order:
            entering[c] = state_t.astype(BF16)
            state_t = state_t * dec[c, :, sl] + upd_t[c]
        st_ref[h] = state_t
        for pr in range(nc // 2):
            rows = slice(pr * P, (pr + 1) * P)
            o_pair = _dot(scores[pr], v_ref[rows, sl])
            for half in range(2):
                c = 2 * pr + half
                rows_c = slice(c * C, (c + 1) * C)
                o = o_pair[half * C:(half + 1) * C] + lax.dot_general(
                    q_st[rows_c, sl], entering[c], NT_DIMS, preferred_element_type=F32)
                if final:
                    tot = o + prev_ref[rows_c, sl]
                    ms = jnp.mean(tot * tot, axis=-1, keepdims=True)
                    y = tot * lax.rsqrt(ms + RMS_EPS) * gain_ref[:, sl] * g_ref[rows_c, sl].astype(F32)
                    o_ref[rows_c, sl] = y.astype(o_ref.dtype)
                else:
                    o_ref[rows_c, sl] = o


def _hgrn_pass(qh, z, vg, lb, tri, batch, seq, *, reverse, prev=None, gain=None, nc, hb):
    m, hv = qh.shape
    T, W = nc * HG_CHUNK, hb * HG_DIM
    nt, nh = seq // T, hv // W
    final = prev is not None

    def tmap(b, h, n):
        return b * nt + ((nt - 1 - n) if reverse else n)

    zoff = nh if reverse else 0
    in_specs = [pl.BlockSpec((T, W), lambda b, h, n: (tmap(b, h, n), h)),
                pl.BlockSpec((T, W), lambda b, h, n: (tmap(b, h, n), h + zoff)),
                pl.BlockSpec((T, W), lambda b, h, n: (tmap(b, h, n), h)),
                pl.BlockSpec((1, W), lambda b, h, n: (0, h)),
                pl.BlockSpec((TRI_BLOCK, TRI_BLOCK), lambda b, h, n: (0, 0))]
    args = [qh, z, vg, lb, tri]
    if final:
        in_specs += [pl.BlockSpec((T, W), lambda b, h, n: (tmap(b, h, n), h)),
                     pl.BlockSpec((T, W), lambda b, h, n: (tmap(b, h, n), h + nh)),
                     pl.BlockSpec((1, W), lambda b, h, n: (0, h))]
        args += [prev, vg, gain]
    return pl.pallas_call(
        functools.partial(_hgrn_kernel, reverse=reverse, final=final, nc=nc, hb=hb),
        grid=(batch, nh, nt),
        in_specs=in_specs,
        out_specs=pl.BlockSpec((T, W), lambda b, h, n: (tmap(b, h, n), h)),
        out_shape=jax.ShapeDtypeStruct((m, hv), BF16 if final else F32),
        scratch_shapes=[pltpu.VMEM((hb, HG_DIM, HG_DIM), F32)],
        compiler_params=_params("parallel", "parallel", "arbitrary"),
        name="hgrn2_bwd_merge" if final else "hgrn2_fwd",
    )(*args)


def _chunk_triangle(reverse):
    i = jnp.arange(TRI_BLOCK)
    same = (i[:, None] // HG_CHUNK) == (i[None, :] // HG_CHUNK)
    tri = (i[None, :] >= i[:, None]) if reverse else (i[None, :] <= i[:, None])
    return (same & tri).astype(BF16)


def _mix_kernel(x_ref, ya_ref, yh_ref, wga_ref, wgh_ref, wpa_ref, wpb_ref, bga_ref, bgh_ref, o_ref):
    x = x_ref[...]
    ga = _sigmoid(_dot(x, wga_ref[...]) + bga_ref[...])
    gh = _sigmoid(_dot(x, wgh_ref[...]) + bgh_ref[...])
    o = ga * _dot(ya_ref[...], wpa_ref[...]) + gh * _dot(yh_ref[...], wpb_ref[...])
    o_ref[...] = o.astype(o_ref.dtype)


def _mix(xb, ya, yh, w_gate_b, b_gate, w_pa_b, w_pb_b, *, tm, tn):
    m, d = xb.shape
    da, dh = ya.shape[1], yh.shape[1]
    nj = d // tn
    return pl.pallas_call(
        _mix_kernel,
        grid=(m // tm, nj),
        in_specs=[pl.BlockSpec((tm, d), lambda i, j: (i, 0)),
                  pl.BlockSpec((tm, da), lambda i, j: (i, 0)),
                  pl.BlockSpec((tm, dh), lambda i, j: (i, 0)),
                  pl.BlockSpec((d, tn), lambda i, j: (0, j)),
                  pl.BlockSpec((d, tn), lambda i, j: (0, j + nj)),
                  pl.BlockSpec((da, tn), lambda i, j: (0, j)),
                  pl.BlockSpec((dh, tn), lambda i, j: (0, j)),
                  pl.BlockSpec((1, tn), lambda i, j: (0, j)),
                  pl.BlockSpec((1, tn), lambda i, j: (0, j + nj))],
        out_specs=pl.BlockSpec((tm, tn), lambda i, j: (i, j)),
        out_shape=jax.ShapeDtypeStruct((m, d), BF16),
        compiler_params=_params("parallel", "arbitrary"),
        name="gated_branch_merge",
    )(xb, ya, yh, w_gate_b, w_gate_b, w_pa_b, w_pb_b, b_gate, b_gate)


def _mm_res_kernel(a_ref, w_ref, res_ref, o_ref, *, alpha):
    o_ref[...] = alpha * res_ref[...] + _dot(a_ref[...], w_ref[...])


def _mm_res(a, wb, res3, alpha, *, tm, name):
    m, k = a.shape
    nj, _, tn = res3.shape
    return pl.pallas_call(
        functools.partial(_mm_res_kernel, alpha=alpha),
        grid=(m // tm, nj),
        in_specs=[pl.BlockSpec((tm, k), lambda i, j: (i, 0)),
                  pl.BlockSpec((k, tn), lambda i, j: (0, j)),
                  pl.BlockSpec((None, tm, tn), lambda i, j: (j, i, 0))],
        out_specs=pl.BlockSpec((tm, tn), lambda i, j: (i, j)),
        out_shape=jax.ShapeDtypeStruct((m, nj * tn), F32),
        compiler_params=_params("parallel", "arbitrary"),
        name=name,
    )(a, wb, res3)


def _layer_norm_blocks(load, store, g_ref, b_ref, rows, groups, rb):
    width = groups * LANES
    for r in range(rows // rb):
        rs = slice(r * rb, (r + 1) * rb)

        def row_sum(fn):
            acc = fn(load(rs, 0))
            for k in range(1, groups):
                acc = acc + fn(load(rs, k))
            return jnp.sum(acc, axis=-1, keepdims=True)

        mu = row_sum(lambda t: t) / width
        inv = lax.rsqrt(row_sum(lambda t: (t - mu) * (t - mu)) / width + LN_EPS)
        for k in range(groups):
            cols = slice(k * LANES, (k + 1) * LANES)
            store(rs, k, (load(rs, k) - mu) * inv * g_ref[:, cols] + b_ref[:, cols])


def _mm_res_ln_kernel(a_ref, w_ref, res_ref, g_ref, b_ref, o_ref, ob_ref, *, alpha):
    j = pl.program_id(1)
    nj, _, tn = o_ref.shape
    o_ref[j] = alpha * res_ref[...] + _dot(a_ref[...], w_ref[...])

    @pl.when(j == nj - 1)
    def _():
        per = tn // LANES
        at = lambda k: slice((k % per) * LANES, (k % per + 1) * LANES)

        def store(rs, k, y):
            o_ref[k // per, rs, at(k)] = y
            ob_ref[rs, k * LANES:(k + 1) * LANES] = y.astype(ob_ref.dtype)

        _layer_norm_blocks(lambda rs, k: o_ref[k // per, rs, at(k)], store, g_ref, b_ref,
                           o_ref.shape[1], nj * per, LN_ROW_BLOCK)


def _mm_res_ln(a, wb, res, alpha, g, b, *, tm, tn, name):
    m, k = a.shape
    n = wb.shape[1]
    nj = n // tn
    vec = pl.BlockSpec((1, n), lambda i, j: (0, 0))
    return pl.pallas_call(
        functools.partial(_mm_res_ln_kernel, alpha=alpha),
        grid=(m // tm, nj),
        in_specs=[pl.BlockSpec((tm, k), lambda i, j: (i, 0)),
                  pl.BlockSpec((k, tn), lambda i, j: (0, j)),
                  pl.BlockSpec((tm, tn), lambda i, j: (i, j)),
                  vec, vec],
        out_specs=[pl.BlockSpec((nj, tm, tn), lambda i, j: (0, i, 0)),
                   pl.BlockSpec((tm, n), lambda i, j: (i, 0))],
        out_shape=[jax.ShapeDtypeStruct((nj, m, tn), F32), jax.ShapeDtypeStruct((m, n), BF16)],
        compiler_params=_params("parallel", "arbitrary"),
        name=name,
    )(a, wb, res, g, b)


def _ffn_up_kernel(*refs, blocks_per_seq, parts, n_cast):
    x_ref, xp_ref, xn_ref, wu_ref, wg_ref, cw_ref, cb_ref = refs[:7]
    o_ref = refs[7 + n_cast]
    xh_ref, g_ref = refs[8 + 2 * n_cast:]
    _cast_in_kernel(refs[7:7 + n_cast], refs[8 + n_cast:8 + 2 * n_cast])
    i, j = pl.program_id(0), pl.program_id(1)
    tm = x_ref.shape[0]
    halo = xp_ref.shape[0]

    @pl.when(j == 0)
    def _():
        first = (i % blocks_per_seq) == 0
        last = (i % blocks_per_seq) == blocks_per_seq - 1
        xh_ref[0:halo] = jnp.where(first, jnp.zeros_like(xp_ref), xp_ref[...])
        xh_ref[halo:halo + tm] = x_ref[...]
        xh_ref[halo + tm:] = jnp.where(last, jnp.zeros_like(xn_ref), xn_ref[...])

    rows = tm // parts
    w = cw_ref[...]

    def matmuls(s):
        lo = 0 if s == 0 else halo + s * rows
        hi = tm + 2 * halo if s == parts - 1 else halo + (s + 1) * rows
        g_ref[lo:hi] = _dot(xh_ref[lo:hi], wg_ref[...])
        return _dot(xh_ref[halo + s * rows:halo + (s + 1) * rows], wu_ref[...])

    def gate(s, u):
        win = g_ref[halo + s * rows - 8:halo + (s + 1) * rows + 8]
        g_prev = pltpu.roll(win, 1, 0)[8:8 + rows]
        g_next = pltpu.roll(win, rows + 15, 0)[8:8 + rows]
        gc = g_prev * w[0:1] + win[8:8 + rows] * w[1:2] + g_next * w[2:3] + cb_ref[...]
        o_ref[s * rows:(s + 1) * rows] = (_silu(gc) * u).astype(o_ref.dtype)

    u_prev = matmuls(0)
    for s in range(1, parts):
        u_next = matmuls(s)
        gate(s - 1, u_prev)
        u_prev = u_next
    gate(parts - 1, u_prev)


def _ffn_up(xb, w_up_b, conv_w, conv_b, weights, seq, *, tm, tn, parts):
    m, d = xb.shape
    dff = w_up_b.shape[1] // 2
    nj = dff // tn
    halo = 16
    rb = tm // halo
    last_halo = m // halo - 1
    flat, slabs, rides = _cast_slabs(weights, (m // tm) * nj, lambda i, j: i * nj + j)
    out = pl.pallas_call(
        functools.partial(_ffn_up_kernel, blocks_per_seq=seq // tm, parts=parts, n_cast=len(flat)),
        grid=(m // tm, nj),
        in_specs=[pl.BlockSpec((tm, d), lambda i, j: (i, 0), pipeline_mode=pl.Buffered(1)),
                  pl.BlockSpec((halo, d), lambda i, j: (jnp.maximum(i * rb - 1, 0), 0)),
                  pl.BlockSpec((halo, d), lambda i, j: (jnp.minimum((i + 1) * rb, last_halo), 0)),
                  pl.BlockSpec((d, tn), lambda i, j: (0, j)),
                  pl.BlockSpec((d, tn), lambda i, j: (0, j + nj)),
                  pl.BlockSpec((3, tn), lambda i, j: (0, j)),
                  pl.BlockSpec((1, tn), lambda i, j: (0, j))] + slabs,
        out_specs=[pl.BlockSpec((tm, tn), lambda i, j: (i, j))] + slabs,
        out_shape=[jax.ShapeDtypeStruct((m, dff), BF16)] + [jax.ShapeDtypeStruct(f.shape, BF16) for f in flat],
        scratch_shapes=[pltpu.VMEM((tm + 2 * halo, d), BF16),
                        pltpu.VMEM((tm + 2 * halo, tn), F32)],
        compiler_params=_params("parallel", "arbitrary"),
        name="ffn_up_conv_gate",
    )(xb, xb, xb, w_up_b, w_up_b, conv_w, conv_b, *flat)
    return out[0], _cast_results(weights, rides, out[1:])


def _ln_ple_kernel(z_ref, g_ref, b_ref, pb_ref, wpg_ref, wple_ref, o_ref, x_ref, xb_ref):
    j = pl.program_id(1)
    nj, _, tn = x_ref.shape

    @pl.when(j == 0)
    def _():
        per = tn // LANES

        def store(rs, k, y):
            x_ref[k // per, rs, (k % per) * LANES:(k % per + 1) * LANES] = y
            xb_ref[rs, k * LANES:(k + 1) * LANES] = y.astype(xb_ref.dtype)

        _layer_norm_blocks(lambda rs, k: z_ref[rs, k * LANES:(k + 1) * LANES], store, g_ref, b_ref,
                           z_ref.shape[0], nj * per, LN_ROW_BLOCK)

    gate = _sigmoid(_dot(xb_ref[...], wpg_ref[...]))
    o_ref[...] = x_ref[j] + gate * _dot(pb_ref[...], wple_ref[...])


def _ln_ple(z, g, b, pb, w_pg_b, w_ple_b, *, tm, tn):
    m, d = z.shape
    dp = pb.shape[1]
    vec = pl.BlockSpec((1, d), lambda i, j: (0, 0))
    return pl.pallas_call(
        _ln_ple_kernel,
        grid=(m // tm, d // tn),
        in_specs=[pl.BlockSpec((tm, d), lambda i, j: (i, 0)),
                  vec, vec,
                  pl.BlockSpec((tm, dp), lambda i, j: (i, 0)),
                  pl.BlockSpec((d, tn), lambda i, j: (0, j)),
                  pl.BlockSpec((dp, tn), lambda i, j: (0, j))],
        out_specs=pl.BlockSpec((tm, tn), lambda i, j: (i, j)),
        out_shape=jax.ShapeDtypeStruct((m, d), F32),
        scratch_shapes=[pltpu.VMEM((d // tn, tm, tn), F32), pltpu.VMEM((tm, d), BF16)],
        compiler_params=_params("parallel", "arbitrary"),
        name="ln_ple_gate",
    )(z, g, b, pb, w_pg_b, w_ple_b)


def _rope_tables(seq):
    pos = jnp.arange(seq)
    row = (pos // GRID_W).astype(F32)
    col = (pos % GRID_W).astype(F32)
    sec = HEAD_DIM // 2
    inv = ROPE_THETA ** (-jnp.arange(0, sec, 2, dtype=F32) / sec)
    ang_r = row[:, None] * inv[None, :]
    ang_c = col[:, None] * inv[None, :]
    ang = jnp.concatenate([ang_r, ang_r, ang_c, ang_c], axis=-1)
    cos, sin = jnp.cos(ang), jnp.sin(ang)
    first_half = (jnp.arange(HEAD_DIM) % sec) < sec // 2
    sa = jnp.where(first_half[None, :], -sin, 0.0)
    sb = jnp.where(first_half[None, :], 0.0, sin)
    return cos, sa, sb


def _qk_rotary_tables(rope, q_gain, k_gain):
    cos, sa, sb = rope
    quarter = HEAD_DIM // 4

    def fold(gain, scale):
        g = gain.astype(F32) * scale
        return (cos * g[None, :], sa * jnp.roll(g, HEAD_DIM - quarter)[None, :], sb * jnp.roll(g, quarter)[None, :])

    tq = fold(q_gain, HEAD_DIM ** -0.5 * LOG2_E)
    tk = fold(k_gain, 1.0)
    return tuple(jnp.stack([a, b]) for a, b in zip(tq, tk))


def _layer(x, p, w_in, q_norm, k_norm, lb_f, lb_b, hg_norm, w_pa, w_pb, w_gate, b_gate, w_o,
           ln1_g, ln1_b, w_up, conv_w, conv_b, w_down, ln2_g, ln2_b, w_pg, w_ple,
           batch, seq, alpha, tables, tris):
    m, d = x.shape
    attn_dim = N_Q_HEADS * HEAD_DIM
    kv_dim = N_KV_HEADS * HEAD_DIM
    hq = HG_HEADS * HG_DIM
    row2 = lambda v: v.reshape(1, -1).astype(F32)
    bf = lambda v: v.astype(BF16)

    xb = bf(x)
    w_in_b = bf(w_in)
    tm = min(1024, seq)

    qkv = _qkv_proj(xb, w_in_b, _qk_rotary_tables(tables, q_norm, k_norm), seq, tm=tm, tn=512, parts=2)
    c0 = attn_dim + 2 * kv_dim
    qh = _proj(xb, w_in_b, c0, hq, 0, BF16, tm=tm, tn=512, parts=2, name="hgrn_q_proj")
    z = _proj(xb, w_in_b, c0 + hq, 2 * hq, 2 * hq // 512, F32, tm=tm, tn=512, parts=2, name="hgrn_gate_proj")
    vg = _proj(xb, w_in_b, c0 + 3 * hq, 2 * hq, hq // 512, BF16, tm=tm, tn=512, parts=2, name="hgrn_vg_proj")

    v = qkv[:, attn_dim + kv_dim:]
    vt = v.reshape(batch, seq, kv_dim).transpose(0, 2, 1).reshape(batch * kv_dim, seq)
    later_weights = [w_gate, w_pa, w_pb, w_o, w_up, w_pg, w_ple]
    y_attn, (w_gate_b, w_pa_b, w_pb_b, w_o_b, w_up_b, w_pg_b, w_ple_b) = _attention(
        qkv, vt, later_weights, batch, seq, tq=min(1024, seq), tk=min(2048, seq))

    nc = min(16, seq // HG_CHUNK)
    o_fwd = _hgrn_pass(qh, z, vg, row2(lb_f), tris[0], batch, seq, reverse=False, nc=nc, hb=4)
    y_hgrn = _hgrn_pass(qh, z, vg, row2(lb_b), tris[1], batch, seq, reverse=True,
                        prev=o_fwd, gain=row2(hg_norm), nc=nc, hb=4)

    mixed = _mix(xb, y_attn, y_hgrn, w_gate_b, row2(b_gate), w_pa_b, w_pb_b, tm=min(512, seq), tn=512)
    x1_blocked, x1b = _mm_res_ln(mixed, w_o_b, x, alpha, row2(ln1_g), row2(ln1_b),
                                 tm=min(512, seq), tn=512, name="attn_out_residual_ln")

    act, (w_down_b,) = _ffn_up(x1b, w_up_b, conv_w.astype(F32), row2(conv_b), [w_down], seq,
                               tm=min(2048, seq), tn=256, parts=4)
    z2 = _mm_res(act, w_down_b, x1_blocked, alpha, tm=min(512, seq), name="ffn_down_residual")

    return _ln_ple(z2, row2(ln2_g), row2(ln2_b), bf(p), w_pg_b, w_ple_b, tm=min(512, seq), tn=512)


def kernel(x, p, w_in, q_norm, k_norm, lb_logits, hg_norm, w_pa, w_pb, w_gate, b_gate, w_o, ln1_g, ln1_b,
           w_up, conv_w, conv_b, w_down, ln2_g, ln2_b, w_pg, w_ple):
    batch, seq, d = x.shape
    depth = w_in.shape[0]
    alpha = (2.0 * depth) ** 0.25
    tables = _rope_tables(seq)
    tris = (_chunk_triangle(False), _chunk_triangle(True))
    lb_all = jnp.cumsum(jax.nn.softmax(lb_logits.astype(F32), axis=1), axis=1)
    h = x.reshape(batch * seq, d)
    for i in range(depth):
        h = _layer(h, p[i].reshape(batch * seq, -1), w_in[i], q_norm[i], k_norm[i], lb_all[0, i], lb_all[1, i],
                   hg_norm[i], w_pa[i], w_pb[i], w_gate[i], b_gate[i], w_o[i], ln1_g[i], ln1_b[i],
                   w_up[i], conv_w[i], conv_b[i], w_down[i], ln2_g[i], ln2_b[i], w_pg[i], w_ple[i],
                   batch, seq, alpha, tables, tris)
    return h.reshape(batch, seq, d)
```

```python
import functools

import jax
import jax.numpy as jnp
from jax import lax
from jax.experimental import pallas as pl
from jax.experimental.pallas import tpu as pltpu

F32 = jnp.float32
BF16 = jnp.bfloat16

GRID_W = 64
HEAD_DIM = 128
N_Q_HEADS = 16
N_KV_HEADS = 4
Q_PER_KV = N_Q_HEADS // N_KV_HEADS
ROPE_THETA = 10000.0
HG_HEADS = 16
HG_DIM = 128
HG_CHUNK = 64
RMS_EPS = 1e-6
LN_EPS = 1e-5

LOG2_E = 1.4426950408889634
LANES = 128
SUBLANES = 8
BF16_ROWS = 16
TRI_BLOCK = 256
LN_ROW_BLOCK = 64
VMEM_LIMIT = 56 * 1024 * 1024

NT_DIMS = (((1,), (1,)), ((), ()))
TN_DIMS = (((0,), (0,)), ((), ()))


def _params(*sem):
    return pltpu.CompilerParams(dimension_semantics=sem, vmem_limit_bytes=VMEM_LIMIT)


def _dot(a, b):
    return jnp.dot(a, b, preferred_element_type=F32)


def _sigmoid(v):
    return 0.5 + 0.5 * jnp.tanh(0.5 * v)


def _silu(v):
    h = 0.5 * v
    return h + h * jnp.tanh(h)


def _cast_slabs(weights, steps, step_index):
    tile = BF16_ROWS

    def spec(w):
        ntiles = w.shape[0] // tile
        if w.shape[0] % tile or (ntiles % steps and steps % ntiles):
            return None
        per_step, hold = max(ntiles // steps, 1), max(steps // ntiles, 1)
        return pl.BlockSpec((tile * per_step, w.shape[1]), lambda *ids: (step_index(*ids) // hold, 0))

    specs = [spec(w) for w in weights]
    rides = [s is not None for s in specs]
    return [w for w, r in zip(weights, rides) if r], [s for s in specs if s is not None], rides


def _cast_in_kernel(w_refs, wb_refs):
    for w_ref, wb_ref in zip(w_refs, wb_refs):
        wb_ref[...] = w_ref[...].astype(wb_ref.dtype)


def _cast_results(weights, rides, cast_outputs):
    cast = iter(cast_outputs)
    return [next(cast) if r else w.astype(BF16) for w, r in zip(weights, rides)]


def _matmul_row_parts(x_ref, w_ref, parts, epilogue):
    rows = x_ref.shape[0] // parts
    part = lambda s: slice(s * rows, (s + 1) * rows)
    acc_prev = _dot(x_ref[part(0)], w_ref[...])
    for s in range(1, parts):
        acc = _dot(x_ref[part(s)], w_ref[...])
        epilogue(part(s - 1), acc_prev)
        acc_prev = acc
    epilogue(part(parts - 1), acc_prev)


def _qkv_kernel(x_ref, w_ref, c0_ref, c1_ref, c2_ref, o_ref, *, nqb, nkb, parts):
    j = pl.program_id(1)
    quarter = HEAD_DIM // 4

    def norm_rope(rs, acc):
        c0, c1, c2 = c0_ref[rs], c1_ref[rs], c2_ref[rs]
        for h in range(acc.shape[1] // HEAD_DIM):
            sl = slice(h * HEAD_DIM, (h + 1) * HEAD_DIM)
            blk = acc[:, sl]
            inv = lax.rsqrt(jnp.mean(blk * blk, axis=-1, keepdims=True) + RMS_EPS)
            rot = blk * c0 + pltpu.roll(blk, HEAD_DIM - quarter, 1) * c1 + pltpu.roll(blk, quarter, 1) * c2
            o_ref[rs, sl] = (rot * inv).astype(o_ref.dtype)

    def passthrough(rs, acc):
        o_ref[rs] = acc.astype(o_ref.dtype)

    pl.when(j < nqb + nkb)(lambda: _matmul_row_parts(x_ref, w_ref, parts, norm_rope))
    pl.when(j >= nqb + nkb)(lambda: _matmul_row_parts(x_ref, w_ref, parts, passthrough))


def _qkv_proj(xb, w_in_b, tables, seq, *, tm, tn, parts):
    m, d = xb.shape
    attn_dim = N_Q_HEADS * HEAD_DIM
    kv_dim = N_KV_HEADS * HEAD_DIM
    n = attn_dim + 2 * kv_dim
    nsb = seq // tm
    nqb = attn_dim // tn
    tab = pl.BlockSpec((None, tm, HEAD_DIM), lambda i, j: ((j >= nqb).astype(jnp.int32), i % nsb, 0))
    return pl.pallas_call(
        functools.partial(_qkv_kernel, nqb=nqb, nkb=kv_dim // tn, parts=parts),
        grid=(m // tm, n // tn),
        in_specs=[pl.BlockSpec((tm, d), lambda i, j: (i, 0)),
                  pl.BlockSpec((d, tn), lambda i, j: (0, j)),
                  tab, tab, tab],
        out_specs=pl.BlockSpec((tm, tn), lambda i, j: (i, j)),
        out_shape=jax.ShapeDtypeStruct((m, n), BF16),
        compiler_params=_params("parallel", "arbitrary"),
        name="qkv_proj",
    )(xb, w_in_b, *tables)


def _proj_kernel(x_ref, w_ref, o_ref, *, silu_from, parts):
    j = pl.program_id(1)

    def store(fn):
        def epilogue(rs, acc):
            o_ref[rs] = fn(acc).astype(o_ref.dtype)
        return epilogue

    pl.when(j >= silu_from)(lambda: _matmul_row_parts(x_ref, w_ref, parts, store(_silu)))
    pl.when(j < silu_from)(lambda: _matmul_row_parts(x_ref, w_ref, parts, store(lambda acc: acc)))


def _proj(xb, wb, col0, ncols, silu_from, out_dtype, *, tm, tn, parts, name):
    m, d = xb.shape
    off = col0 // tn
    return pl.pallas_call(
        functools.partial(_proj_kernel, silu_from=silu_from, parts=parts),
        grid=(m // tm, ncols // tn),
        in_specs=[pl.BlockSpec((tm, d), lambda i, j: (i, 0)),
                  pl.BlockSpec((d, tn), lambda i, j: (0, j + off))],
        out_specs=pl.BlockSpec((tm, tn), lambda i, j: (i, j)),
        out_shape=jax.ShapeDtypeStruct((m, ncols), out_dtype),
        compiler_params=_params("parallel", "arbitrary"),
        name=name,
    )(xb, wb)


def _attn_kernel(*refs, n_cast):
    q_ref, k_ref, kn_ref, vt_ref = refs[:4]
    w_refs = refs[4:4 + n_cast]
    o_ref = refs[4 + n_cast]
    wb_refs = refs[5 + n_cast:5 + 2 * n_cast]
    m_ref, l_ref, acc_ref, s0_ref = refs[5 + 2 * n_cast:]
    ki = pl.program_id(3)

    _cast_in_kernel(w_refs, wb_refs)

    def scores_t(keys, g):
        return lax.dot_general(keys, q_ref[:, g * HEAD_DIM:(g + 1) * HEAD_DIM], NT_DIMS,
                               preferred_element_type=F32)

    @pl.when(ki == 0)
    def _():
        m_ref[...] = jnp.full_like(m_ref, -1e30)
        l_ref[...] = jnp.zeros_like(l_ref)
        acc_ref[...] = jnp.zeros_like(acc_ref)
        s0_ref[...] = scores_t(k_ref[...], 0)

    k = k_ref[...]
    vt = vt_ref[...]

    st_next = s0_ref[...]
    for g in range(Q_PER_KV):
        row = slice(g, g + 1)
        st = st_next
        if g + 1 < Q_PER_KV:
            st_next = scores_t(k, g + 1)
        else:
            s0_ref[...] = scores_t(kn_ref[...], 0)
        m_old = m_ref[row, :]
        m_new = jnp.maximum(m_old, jnp.max(st, axis=0, keepdims=True))
        alpha = jnp.exp2(m_old - m_new)
        p = jnp.exp2(st - m_new)
        l_ref[row, :] = alpha * l_ref[row, :] + jnp.sum(p, axis=0, keepdims=True)
        acc_ref[g] = alpha * acc_ref[g] + _dot(vt, p.astype(BF16))
        m_ref[row, :] = m_new

    @pl.when(ki == pl.num_programs(3) - 1)
    def _():
        for g in range(Q_PER_KV):
            o = acc_ref[g] / l_ref[g:g + 1, :]
            o_ref[:, g * HEAD_DIM:(g + 1) * HEAD_DIM] = o.T.astype(o_ref.dtype)


def _attention(qkv, vt, weights, batch, seq, *, tq, tk):
    m = qkv.shape[0]
    gw = Q_PER_KV * HEAD_DIM
    nq, nk = seq // tq, seq // tk
    k_col0 = N_Q_HEADS
    step = lambda b, h, qi, ki: ((b * N_KV_HEADS + h) * nq + qi) * nk + ki
    flat, slabs, rides = _cast_slabs(weights, batch * N_KV_HEADS * nq * nk, step)
    out = pl.pallas_call(
        functools.partial(_attn_kernel, n_cast=len(flat)),
        grid=(batch, N_KV_HEADS, nq, nk),
        in_specs=[pl.BlockSpec((tq, gw), lambda b, h, qi, ki: (b * nq + qi, h)),
                  pl.BlockSpec((tk, HEAD_DIM), lambda b, h, qi, ki: (b * nk + ki, k_col0 + h)),
                  pl.BlockSpec((tk, HEAD_DIM),
                               lambda b, h, qi, ki: (b * nk + jnp.minimum(ki + 1, nk - 1), k_col0 + h)),
                  pl.BlockSpec((HEAD_DIM, tk), lambda b, h, qi, ki: (b * N_KV_HEADS + h, ki))] + slabs,
        out_specs=[pl.BlockSpec((tq, gw), lambda b, h, qi, ki: (b * nq + qi, h))] + slabs,
        out_shape=[jax.ShapeDtypeStruct((m, N_Q_HEADS * HEAD_DIM), BF16)]
        + [jax.ShapeDtypeStruct(f.shape, BF16) for f in flat],
        scratch_shapes=[pltpu.VMEM((SUBLANES, tq), F32), pltpu.VMEM((SUBLANES, tq), F32),
                        pltpu.VMEM((Q_PER_KV, HEAD_DIM, tq), F32), pltpu.VMEM((tk, tq), F32)],
        compiler_params=_params("parallel", "parallel", "parallel", "arbitrary"),
        name="gqa_attention",
    )(qkv, qkv, qkv, vt, *flat)
    return out[0], _cast_results(weights, rides, out[1:])


def _hgrn_kernel(*refs, reverse, final, nc, hb):
    if final:
        q_ref, z_ref, v_ref, lb_ref, tri_ref, prev_ref, g_ref, gain_ref, o_ref, st_ref = refs
    else:
        q_ref, z_ref, v_ref, lb_ref, tri_ref, o_ref, st_ref = refs
    C, D = HG_CHUNK, HG_DIM
    T, W = nc * C, hb * D

    @pl.when(pl.program_id(2) == 0)
    def _():
        st_ref[...] = jnp.zeros_like(st_ref)

    lb = lb_ref[...]
    half_span = 0.5 * (1.0 - lb)
    g = half_span * jnp.tanh(0.5 * z_ref[...])
    key = half_span - g
    logf = jnp.log2(0.5 * (1.0 + lb) + g)

    hi = logf.astype(BF16)
    r1 = logf - hi.astype(F32)
    mid = r1.astype(BF16)
    lo = (r1 - mid.astype(F32)).astype(BF16)
    cat = jnp.concatenate([hi, mid, lo], axis=1)
    tri = tri_ref[...]
    parts = []
    for s in range(T // TRI_BLOCK):
        bc = _dot(tri, cat[s * TRI_BLOCK:(s + 1) * TRI_BLOCK, :])
        parts.append((bc[:, :W] + bc[:, W:2 * W]) + bc[:, 2 * W:])
    b = parts[0] if len(parts) == 1 else jnp.concatenate(parts, axis=0)
    b3 = b.reshape(nc, C, W)
    ref_row = C // 2 if reverse else C // 2 - 1
    end_row = 0 if reverse else C - 1
    bref = b3[:, ref_row:ref_row + 1, :]
    btot = b3[:, end_row:end_row + 1, :]

    q3 = q_ref[...].astype(F32).reshape(nc, C, W)
    q_in = q3 * jnp.exp2(b3 - bref)
    k_in = key.reshape(nc, C, W) * jnp.exp2(bref - b3)
    q_st = (q_in * jnp.exp2(bref)).astype(BF16).reshape(T, W)
    k_up = (k_in * jnp.exp2(btot - bref)).astype(BF16).reshape(T, W)
    dec = jnp.exp2(btot)
    q_in = q_in.astype(BF16).reshape(T, W)
    k_in = k_in.astype(BF16).reshape(T, W)

    P = 2 * C
    rowi = lax.broadcasted_iota(jnp.int32, (P, P), 0)
    coli = lax.broadcasted_iota(jnp.int32, (P, P), 1)
    mask = ((rowi >= C) == (coli >= C)) & ((coli >= rowi) if reverse else (coli <= rowi))
    order = range(nc - 1, -1, -1) if reverse else range(nc)
    for h in range(hb):
        sl = slice(h * D, (h + 1) * D)
        scores = []
        for pr in range(nc // 2):
            rows = slice(pr * P, (pr + 1) * P)
            a = lax.dot_general(q_in[rows, sl], k_in[rows, sl], NT_DIMS, preferred_element_type=F32)
            scores.append(jnp.where(mask, a, 0.0).astype(BF16))
        upd_t = [lax.dot_general(v_ref[c * C:(c + 1) * C, sl], k_up[c * C:(c + 1) * C, sl], TN_DIMS,
                                 preferred_element_type=F32) for c in range(nc)]
        state_t = st_ref[h]
        entering = [None] * nc
        for c in order:
            entering[c] = state_t.astype(BF16)
            state_t = state_t * dec[c, :, sl] + upd_t[c]
        st_ref[h] = state_t
        for pr in range(nc // 2):
            rows = slice(pr * P, (pr + 1) * P)
            o_pair = _dot(scores[pr], v_ref[rows, sl])
            for half in range(2):
                c = 2 * pr + half
                rows_c = slice(c * C, (c + 1) * C)
                o = o_pair[half * C:(half + 1) * C] + lax.dot_general(
                    q_st[rows_c, sl], entering[c], NT_DIMS, preferred_element_type=F32)
                if final:
                    tot = o + prev_ref[rows_c, sl]
                    ms = jnp.mean(tot * tot, axis=-1, keepdims=True)
                    y = tot * lax.rsqrt(ms + RMS_EPS) * gain_ref[:, sl] * g_ref[rows_c, sl].astype(F32)
                    o_ref[rows_c, sl] = y.astype(o_ref.dtype)
                else:
                    o_ref[rows_c, sl] = o


def _hgrn_pass(qh, z, vg, lb, tri, batch, seq, *, reverse, prev=None, gain=None, nc, hb):
    m, hv = qh.shape
    T, W = nc * HG_CHUNK, hb * HG_DIM
    nt, nh = seq // T, hv // W
    final = prev is not None

    def tmap(b, h, n):
        return b * nt + ((nt - 1 - n) if reverse else n)

    zoff = nh if reverse else 0
    in_specs = [pl.BlockSpec((T, W), lambda b, h, n: (tmap(b, h, n), h)),
                pl.BlockSpec((T, W), lambda b, h, n: (tmap(b, h, n), h + zoff)),
                pl.BlockSpec((T, W), lambda b, h, n: (tmap(b, h, n), h)),
                pl.BlockSpec((1, W), lambda b, h, n: (0, h)),
                pl.BlockSpec((TRI_BLOCK, TRI_BLOCK), lambda b, h, n: (0, 0))]
    args = [qh, z, vg, lb, tri]
    if final:
        in_specs += [pl.BlockSpec((T, W), lambda b, h, n: (tmap(b, h, n), h)),
                     pl.BlockSpec((T, W), lambda b, h, n: (tmap(b, h, n), h + nh)),
                     pl.BlockSpec((1, W), lambda b, h, n: (0, h))]
        args += [prev, vg, gain]
    return pl.pallas_call(
        functools.partial(_hgrn_kernel, reverse=reverse, final=final, nc=nc, hb=hb),
        grid=(batch, nh, nt),
        in_specs=in_specs,
        out_specs=pl.BlockSpec((T, W), lambda b, h, n: (tmap(b, h, n), h)),
        out_shape=jax.ShapeDtypeStruct((m, hv), BF16 if final else F32),
        scratch_shapes=[pltpu.VMEM((hb, HG_DIM, HG_DIM), F32)],
        compiler_params=_params("parallel", "parallel", "arbitrary"),
        name="hgrn2_bwd_merge" if final else "hgrn2_fwd",
    )(*args)


def _chunk_triangle(reverse):
    i = jnp.arange(TRI_BLOCK)
    same = (i[:, None] // HG_CHUNK) == (i[None, :] // HG_CHUNK)
    tri = (i[None, :] >= i[:, None]) if reverse else (i[None, :] <= i[:, None])
    return (same & tri).astype(BF16)


def _mix_kernel(x_ref, ya_ref, yh_ref, wga_ref, wgh_ref, wpa_ref, wpb_ref, bga_ref, bgh_ref, o_ref):
    x = x_ref[...]
    ga = _sigmoid(_dot(x, wga_ref[...]) + bga_ref[...])
    gh = _sigmoid(_dot(x, wgh_ref[...]) + bgh_ref[...])
    o = ga * _dot(ya_ref[...], wpa_ref[...]) + gh * _dot(yh_ref[...], wpb_ref[...])
    o_ref[...] = o.astype(o_ref.dtype)


def _mix(xb, ya, yh, w_gate_b, b_gate, w_pa_b, w_pb_b, *, tm, tn):
    m, d = xb.shape
    da, dh = ya.shape[1], yh.shape[1]
    nj = d // tn
    return pl.pallas_call(
        _mix_kernel,
        grid=(m // tm, nj),
        in_specs=[pl.BlockSpec((tm, d), lambda i, j: (i, 0)),
                  pl.BlockSpec((tm, da), lambda i, j: (i, 0)),
                  pl.BlockSpec((tm, dh), lambda i, j: (i, 0)),
                  pl.BlockSpec((d, tn), lambda i, j: (0, j)),
                  pl.BlockSpec((d, tn), lambda i, j: (0, j + nj)),
                  pl.BlockSpec((da, tn), lambda i, j: (0, j)),
                  pl.BlockSpec((dh, tn), lambda i, j: (0, j)),
                  pl.BlockSpec((1, tn), lambda i, j: (0, j)),
                  pl.BlockSpec((1, tn), lambda i, j: (0, j + nj))],
        out_specs=pl.BlockSpec((tm, tn), lambda i, j: (i, j)),
        out_shape=jax.ShapeDtypeStruct((m, d), BF16),
        compiler_params=_params("parallel", "arbitrary"),
        name="gated_branch_merge",
    )(xb, ya, yh, w_gate_b, w_gate_b, w_pa_b, w_pb_b, b_gate, b_gate)


def _mm_res_kernel(a_ref, w_ref, res_ref, o_ref, *, alpha):
    o_ref[...] = alpha * res_ref[...] + _dot(a_ref[...], w_ref[...])


def _mm_res(a, wb, res3, alpha, *, tm, name):
    m, k = a.shape
    nj, _, tn = res3.shape
    return pl.pallas_call(
        functools.partial(_mm_res_kernel, alpha=alpha),
        grid=(m // tm, nj),
        in_specs=[pl.BlockSpec((tm, k), lambda i, j: (i, 0)),
                  pl.BlockSpec((k, tn), lambda i, j: (0, j)),
                  pl.BlockSpec((None, tm, tn), lambda i, j: (j, i, 0))],
        out_specs=pl.BlockSpec((tm, tn), lambda i, j: (i, j)),
        out_shape=jax.ShapeDtypeStruct((m, nj * tn), F32),
        compiler_params=_params("parallel", "arbitrary"),
        name=name,
    )(a, wb, res3)


def _layer_norm_blocks(load, store, g_ref, b_ref, rows, groups, rb):
    width = groups * LANES
    for r in range(rows // rb):
        rs = slice(r * rb, (r + 1) * rb)

        def row_sum(fn):
            acc = fn(load(rs, 0))
            for k in range(1, groups):
                acc = acc + fn(load(rs, k))
            return jnp.sum(acc, axis=-1, keepdims=True)

        mu = row_sum(lambda t: t) / width
        inv = lax.rsqrt(row_sum(lambda t: (t - mu) * (t - mu)) / width + LN_EPS)
        for k in range(groups):
            cols = slice(k * LANES, (k + 1) * LANES)
            store(rs, k, (load(rs, k) - mu) * inv * g_ref[:, cols] + b_ref[:, cols])


def _mm_res_ln_kernel(a_ref, w_ref, res_ref, g_ref, b_ref, o_ref, ob_ref, *, alpha):
    j = pl.program_id(1)
    nj, _, tn = o_ref.shape
    o_ref[j] = alpha * res_ref[...] + _dot(a_ref[...], w_ref[...])

    @pl.when(j == nj - 1)
    def _():
        per = tn // LANES
        at = lambda k: slice((k % per) * LANES, (k % per + 1) * LANES)

        def store(rs, k, y):
            o_ref[k // per, rs, at(k)] = y
            ob_ref[rs, k * LANES:(k + 1) * LANES] = y.astype(ob_ref.dtype)

        _layer_norm_blocks(lambda rs, k: o_ref[k // per, rs, at(k)], store, g_ref, b_ref,
                           o_ref.shape[1], nj * per, LN_ROW_BLOCK)


def _mm_res_ln(a, wb, res, alpha, g, b, *, tm, tn, name):
    m, k = a.shape
    n = wb.shape[1]
    nj = n // tn
    vec = pl.BlockSpec((1, n), lambda i, j: (0, 0))
    return pl.pallas_call(
        functools.partial(_mm_res_ln_kernel, alpha=alpha),
        grid=(m // tm, nj),
        in_specs=[pl.BlockSpec((tm, k), lambda i, j: (i, 0)),
                  pl.BlockSpec((k, tn), lambda i, j: (0, j)),
                  pl.BlockSpec((tm, tn), lambda i, j: (i, j)),
                  vec, vec],
        out_specs=[pl.BlockSpec((nj, tm, tn), lambda i, j: (0, i, 0)),
                   pl.BlockSpec((tm, n), lambda i, j: (i, 0))],
        out_shape=[jax.ShapeDtypeStruct((nj, m, tn), F32), jax.ShapeDtypeStruct((m, n), BF16)],
        compiler_params=_params("parallel", "arbitrary"),
        name=name,
    )(a, wb, res, g, b)


def _ffn_up_kernel(*refs, blocks_per_seq, parts, n_cast):
    x_ref, xp_ref, xn_ref, wu_ref, wg_ref, cw_ref, cb_ref = refs[:7]
    o_ref = refs[7 + n_cast]
    xh_ref, g_ref = refs[8 + 2 * n_cast:]
    _cast_in_kernel(refs[7:7 + n_cast], refs[8 + n_cast:8 + 2 * n_cast])
    i, j = pl.program_id(0), pl.program_id(1)
    tm = x_ref.shape[0]
    halo = xp_ref.shape[0]

    @pl.when(j == 0)
    def _():
        first = (i % blocks_per_seq) == 0
        last = (i % blocks_per_seq) == blocks_per_seq - 1
        xh_ref[0:halo] = jnp.where(first, jnp.zeros_like(xp_ref), xp_ref[...])
        xh_ref[halo:halo + tm] = x_ref[...]
        xh_ref[halo + tm:] = jnp.where(last, jnp.zeros_like(xn_ref), xn_ref[...])

    rows = tm // parts
    w = cw_ref[...]

    def matmuls(s):
        lo = 0 if s == 0 else halo + s * rows
        hi = tm + 2 * halo if s == parts - 1 else halo + (s + 1) * rows
        g_ref[lo:hi] = _dot(xh_ref[lo:hi], wg_ref[...])
        return _dot(xh_ref[halo + s * rows:halo + (s + 1) * rows], wu_ref[...])

    def gate(s, u):
        pad = SUBLANES
        win = g_ref[halo + s * rows - pad:halo + (s + 1) * rows + pad]
        own = slice(pad, pad + rows)
        g_prev = pltpu.roll(win, 1, 0)[own]
        g_next = pltpu.roll(win, rows + 2 * pad - 1, 0)[own]
        gc = g_prev * w[0:1] + win[own] * w[1:2] + g_next * w[2:3] + cb_ref[...]
        o_ref[s * rows:(s + 1) * rows] = (_silu(gc) * u).astype(o_ref.dtype)

    u_prev = matmuls(0)
    for s in range(1, parts):
        u_next = matmuls(s)
        gate(s - 1, u_prev)
        u_prev = u_next
    gate(parts - 1, u_prev)


def _ffn_up(xb, w_up_b, conv_w, conv_b, weights, seq, *, tm, tn, parts):
    m, d = xb.shape
    dff = w_up_b.shape[1] // 2
    nj = dff // tn
    halo = BF16_ROWS
    rb = tm // halo
    last_halo = m // halo - 1
    flat, slabs, rides = _cast_slabs(weights, (m // tm) * nj, lambda i, j: i * nj + j)
    out = pl.pallas_call(
        functools.partial(_ffn_up_kernel, blocks_per_seq=seq // tm, parts=parts, n_cast=len(flat)),
        grid=(m // tm, nj),
        in_specs=[pl.BlockSpec((tm, d), lambda i, j: (i, 0), pipeline_mode=pl.Buffered(1)),
                  pl.BlockSpec((halo, d), lambda i, j: (jnp.maximum(i * rb - 1, 0), 0)),
                  pl.BlockSpec((halo, d), lambda i, j: (jnp.minimum((i + 1) * rb, last_halo), 0)),
                  pl.BlockSpec((d, tn), lambda i, j: (0, j)),
                  pl.BlockSpec((d, tn), lambda i, j: (0, j + nj)),
                  pl.BlockSpec((3, tn), lambda i, j: (0, j)),
                  pl.BlockSpec((1, tn), lambda i, j: (0, j))] + slabs,
        out_specs=[pl.BlockSpec((tm, tn), lambda i, j: (i, j))] + slabs,
        out_shape=[jax.ShapeDtypeStruct((m, dff), BF16)] + [jax.ShapeDtypeStruct(f.shape, BF16) for f in flat],
        scratch_shapes=[pltpu.VMEM((tm + 2 * halo, d), BF16),
                        pltpu.VMEM((tm + 2 * halo, tn), F32)],
        compiler_params=_params("parallel", "arbitrary"),
        name="ffn_up_conv_gate",
    )(xb, xb, xb, w_up_b, w_up_b, conv_w, conv_b, *flat)
    return out[0], _cast_results(weights, rides, out[1:])


def _ln_ple_kernel(z_ref, g_ref, b_ref, pb_ref, wpg_ref, wple_ref, o_ref, x_ref, xb_ref):
    j = pl.program_id(1)
    nj, _, tn = x_ref.shape

    @pl.when(j == 0)
    def _():
        per = tn // LANES

        def store(rs, k, y):
            x_ref[k // per, rs, (k % per) * LANES:(k % per + 1) * LANES] = y
            xb_ref[rs, k * LANES:(k + 1) * LANES] = y.astype(xb_ref.dtype)

        _layer_norm_blocks(lambda rs, k: z_ref[rs, k * LANES:(k + 1) * LANES], store, g_ref, b_ref,
                           z_ref.shape[0], nj * per, LN_ROW_BLOCK)

    gate = _sigmoid(_dot(xb_ref[...], wpg_ref[...]))
    o_ref[...] = x_ref[j] + gate * _dot(pb_ref[...], wple_ref[...])


def _ln_ple(z, g, b, pb, w_pg_b, w_ple_b, *, tm, tn):
    m, d = z.shape
    dp = pb.shape[1]
    vec = pl.BlockSpec((1, d), lambda i, j: (0, 0))
    return pl.pallas_call(
        _ln_ple_kernel,
        grid=(m // tm, d // tn),
        in_specs=[pl.BlockSpec((tm, d), lambda i, j: (i, 0)),
                  vec, vec,
                  pl.BlockSpec((tm, dp), lambda i, j: (i, 0)),
                  pl.BlockSpec((d, tn), lambda i, j: (0, j)),
                  pl.BlockSpec((dp, tn), lambda i, j: (0, j))],
        out_specs=pl.BlockSpec((tm, tn), lambda i, j: (i, j)),
        out_shape=jax.ShapeDtypeStruct((m, d), F32),
        scratch_shapes=[pltpu.VMEM((d // tn, tm, tn), F32), pltpu.VMEM((tm, d), BF16)],
        compiler_params=_params("parallel", "arbitrary"),
        name="ln_ple_gate",
    )(z, g, b, pb, w_pg_b, w_ple_b)


def _rope_tables(seq):
    pos = jnp.arange(seq)
    row = (pos // GRID_W).astype(F32)
    col = (pos % GRID_W).astype(F32)
    sec = HEAD_DIM // 2
    inv = ROPE_THETA ** (-jnp.arange(0, sec, 2, dtype=F32) / sec)
    ang_r = row[:, None] * inv[None, :]
    ang_c = col[:, None] * inv[None, :]
    ang = jnp.concatenate([ang_r, ang_r, ang_c, ang_c], axis=-1)
    cos, sin = jnp.cos(ang), jnp.sin(ang)
    first_half = (jnp.arange(HEAD_DIM) % sec) < sec // 2
    sa = jnp.where(first_half[None, :], -sin, 0.0)
    sb = jnp.where(first_half[None, :], 0.0, sin)
    return cos, sa, sb


def _qk_rotary_tables(rope, q_gain, k_gain):
    cos, sa, sb = rope
    quarter = HEAD_DIM // 4

    def fold(gain, scale):
        g = gain.astype(F32) * scale
        return (cos * g[None, :], sa * jnp.roll(g, HEAD_DIM - quarter)[None, :], sb * jnp.roll(g, quarter)[None, :])

    tq = fold(q_gain, HEAD_DIM ** -0.5 * LOG2_E)
    tk = fold(k_gain, 1.0)
    return tuple(jnp.stack([a, b]) for a, b in zip(tq, tk))


def _layer(x, p, w_in, q_norm, k_norm, lb_f, lb_b, hg_norm, w_pa, w_pb, w_gate, b_gate, w_o,
           ln1_g, ln1_b, w_up, conv_w, conv_b, w_down, ln2_g, ln2_b, w_pg, w_ple,
           batch, seq, alpha, tables, tris):
    m, d = x.shape
    attn_dim = N_Q_HEADS * HEAD_DIM
    kv_dim = N_KV_HEADS * HEAD_DIM
    hq = HG_HEADS * HG_DIM
    row2 = lambda v: v.reshape(1, -1).astype(F32)
    bf = lambda v: v.astype(BF16)

    xb = bf(x)
    w_in_b = bf(w_in)
    tm = min(1024, seq)

    qkv = _qkv_proj(xb, w_in_b, _qk_rotary_tables(tables, q_norm, k_norm), seq, tm=tm, tn=512, parts=2)
    c0 = attn_dim + 2 * kv_dim
    qh = _proj(xb, w_in_b, c0, hq, 0, BF16, tm=tm, tn=512, parts=2, name="hgrn_q_proj")
    z = _proj(xb, w_in_b, c0 + hq, 2 * hq, 2 * hq // 512, F32, tm=tm, tn=512, parts=2, name="hgrn_gate_proj")
    vg = _proj(xb, w_in_b, c0 + 3 * hq, 2 * hq, hq // 512, BF16, tm=tm, tn=512, parts=2, name="hgrn_vg_proj")

    v = qkv[:, attn_dim + kv_dim:]
    vt = v.reshape(batch, seq, kv_dim).transpose(0, 2, 1).reshape(batch * kv_dim, seq)
    later_weights = [w_gate, w_pa, w_pb, w_o, w_up, w_pg, w_ple]
    y_attn, (w_gate_b, w_pa_b, w_pb_b, w_o_b, w_up_b, w_pg_b, w_ple_b) = _attention(
        qkv, vt, later_weights, batch, seq, tq=min(1024, seq), tk=min(2048, seq))

    nc = min(16, seq // HG_CHUNK)
    o_fwd = _hgrn_pass(qh, z, vg, row2(lb_f), tris[0], batch, seq, reverse=False, nc=nc, hb=4)
    y_hgrn = _hgrn_pass(qh, z, vg, row2(lb_b), tris[1], batch, seq, reverse=True,
                        prev=o_fwd, gain=row2(hg_norm), nc=nc, hb=4)

    mixed = _mix(xb, y_attn, y_hgrn, w_gate_b, row2(b_gate), w_pa_b, w_pb_b, tm=min(512, seq), tn=512)
    x1_blocked, x1b = _mm_res_ln(mixed, w_o_b, x, alpha, row2(ln1_g), row2(ln1_b),
                                 tm=min(512, seq), tn=512, name="attn_out_residual_ln")

    act, (w_down_b,) = _ffn_up(x1b, w_up_b, conv_w.astype(F32), row2(conv_b), [w_down], seq,
                               tm=min(2048, seq), tn=256, parts=4)
    z2 = _mm_res(act, w_down_b, x1_blocked, alpha, tm=min(512, seq), name="ffn_down_residual")

    return _ln_ple(z2, row2(ln2_g), row2(ln2_b), bf(p), w_pg_b, w_ple_b, tm=min(512, seq), tn=512)


def kernel(x, p, w_in, q_norm, k_norm, lb_logits, hg_norm, w_pa, w_pb, w_gate, b_gate, w_o, ln1_g, ln1_b,
           w_up, conv_w, conv_b, w_down, ln2_g, ln2_b, w_pg, w_ple):
    batch, seq, d = x.shape
    depth = w_in.shape[0]
    alpha = (2.0 * depth) ** 0.25
    tables = _rope_tables(seq)
    tris = (_chunk_triangle(False), _chunk_triangle(True))
    lb_all = jnp.cumsum(jax.nn.softmax(lb_logits.astype(F32), axis=1), axis=1)
    h = x.reshape(batch * seq, d)
    for i in range(depth):
        h = _layer(h, p[i].reshape(batch * seq, -1), w_in[i], q_norm[i], k_norm[i], lb_all[0, i], lb_all[1, i],
                   hg_norm[i], w_pa[i], w_pb[i], w_gate[i], b_gate[i], w_o[i], ln1_g[i], ln1_b[i],
                   w_up[i], conv_w[i], conv_b[i], w_down[i], ln2_g[i], ln2_b[i], w_pg[i], w_ple[i],
                   batch, seq, alpha, tables, tris)
    return h.reshape(batch, seq, d)
```

```python
import functools

import jax
import jax.numpy as jnp
from jax import lax
from jax.experimental import pallas as pl
from jax.experimental.pallas import tpu as pltpu

F32 = jnp.float32
BF16 = jnp.bfloat16

GRID_W = 64
HEAD_DIM = 128
N_Q_HEADS = 16
N_KV_HEADS = 4
Q_PER_KV = N_Q_HEADS // N_KV_HEADS
ROPE_THETA = 10000.0
HG_HEADS = 16
HG_DIM = 128
HG_CHUNK = 64
RMS_EPS = 1e-6
LN_EPS = 1e-5

LOG2_E = 1.4426950408889634
LANES = 128
SUBLANES = 8
BF16_ROWS = 16
TRI_BLOCK = 256
LN_ROW_BLOCK = 64
VMEM_LIMIT = 56 * 1024 * 1024

NT_DIMS = (((1,), (1,)), ((), ()))
TN_DIMS = (((0,), (0,)), ((), ()))


def _params(*sem):
    return pltpu.CompilerParams(dimension_semantics=sem, vmem_limit_bytes=VMEM_LIMIT)


def _dot(a, b):
    return jnp.dot(a, b, preferred_element_type=F32)


def _sigmoid(v):
    return 0.5 + 0.5 * jnp.tanh(0.5 * v)


def _silu(v):
    h = 0.5 * v
    return h + h * jnp.tanh(h)


def _cast_slabs(weights, steps, step_index):
    tile = BF16_ROWS

    def spec(w):
        ntiles = w.shape[0] // tile
        if w.shape[0] % tile or (ntiles % steps and steps % ntiles):
            return None
        per_step, hold = max(ntiles // steps, 1), max(steps // ntiles, 1)
        return pl.BlockSpec((tile * per_step, w.shape[1]), lambda *ids: (step_index(*ids) // hold, 0))

    specs = [spec(w) for w in weights]
    rides = [s is not None for s in specs]
    return [w for w, r in zip(weights, rides) if r], [s for s in specs if s is not None], rides


def _cast_in_kernel(w_refs, wb_refs):
    for w_ref, wb_ref in zip(w_refs, wb_refs):
        wb_ref[...] = w_ref[...].astype(wb_ref.dtype)


def _cast_results(weights, rides, cast_outputs):
    cast = iter(cast_outputs)
    return [next(cast) if r else w.astype(BF16) for w, r in zip(weights, rides)]


def _matmul_row_parts(x_ref, w_ref, parts, epilogue):
    rows = x_ref.shape[0] // parts
    part = lambda s: slice(s * rows, (s + 1) * rows)
    acc_prev = _dot(x_ref[part(0)], w_ref[...])
    for s in range(1, parts):
        acc = _dot(x_ref[part(s)], w_ref[...])
        epilogue(part(s - 1), acc_prev)
        acc_prev = acc
    epilogue(part(parts - 1), acc_prev)


def _qkv_kernel(x_ref, w_ref, c0_ref, c1_ref, c2_ref, wr_ref, o_ref, wrb_ref, *, nqb, nkb, parts):
    j = pl.program_id(1)
    quarter = HEAD_DIM // 4
    wrb_ref[...] = wr_ref[...].astype(wrb_ref.dtype)

    def norm_rope(rs, acc):
        c0, c1, c2 = c0_ref[rs], c1_ref[rs], c2_ref[rs]
        for h in range(acc.shape[1] // HEAD_DIM):
            sl = slice(h * HEAD_DIM, (h + 1) * HEAD_DIM)
            blk = acc[:, sl]
            inv = lax.rsqrt(jnp.mean(blk * blk, axis=-1, keepdims=True) + RMS_EPS)
            rot = blk * c0 + pltpu.roll(blk, HEAD_DIM - quarter, 1) * c1 + pltpu.roll(blk, quarter, 1) * c2
            o_ref[rs, sl] = (rot * inv).astype(o_ref.dtype)

    def passthrough(rs, acc):
        o_ref[rs] = acc.astype(o_ref.dtype)

    pl.when(j < nqb + nkb)(lambda: _matmul_row_parts(x_ref, w_ref, parts, norm_rope))
    pl.when(j >= nqb + nkb)(lambda: _matmul_row_parts(x_ref, w_ref, parts, passthrough))


def _qkv_proj(xb, w_in, tables, seq, *, tm, tn, parts):
    m, d = xb.shape
    attn_dim = N_Q_HEADS * HEAD_DIM
    kv_dim = N_KV_HEADS * HEAD_DIM
    n = attn_dim + 2 * kv_dim
    nsb = seq // tm
    nqb = attn_dim // tn
    nj = n // tn
    rest_blocks = (w_in.shape[1] - n) // tn
    row_blocks = (m // tm) * nj // rest_blocks
    assert row_blocks >= 1 and d % (row_blocks * BF16_ROWS) == 0
    rb = d // row_blocks
    tile = lambda i, j: jnp.minimum(i * nj + j, row_blocks * rest_blocks - 1)
    tab = pl.BlockSpec((None, tm, HEAD_DIM), lambda i, j: ((j >= nqb).astype(jnp.int32), i % nsb, 0))
    return pl.pallas_call(
        functools.partial(_qkv_kernel, nqb=nqb, nkb=kv_dim // tn, parts=parts),
        grid=(m // tm, nj),
        in_specs=[pl.BlockSpec((tm, d), lambda i, j: (i, 0)),
                  pl.BlockSpec((d, tn), lambda i, j: (0, j)),
                  tab, tab, tab,
                  pl.BlockSpec((rb, tn), lambda i, j: (tile(i, j) // rest_blocks, nj + tile(i, j) % rest_blocks))],
        out_specs=[pl.BlockSpec((tm, tn), lambda i, j: (i, j)),
                   pl.BlockSpec((rb, tn), lambda i, j: (tile(i, j) // rest_blocks, tile(i, j) % rest_blocks))],
        out_shape=[jax.ShapeDtypeStruct((m, n), BF16), jax.ShapeDtypeStruct((d, rest_blocks * tn), BF16)],
        compiler_params=_params("parallel", "arbitrary"),
        name="qkv_proj",
    )(xb, w_in[:, :n].astype(BF16), *tables, w_in)


def _proj_kernel(x_ref, w_ref, o_ref, *, silu_from, parts):
    j = pl.program_id(1)

    def store(fn):
        def epilogue(rs, acc):
            o_ref[rs] = fn(acc).astype(o_ref.dtype)
        return epilogue

    pl.when(j >= silu_from)(lambda: _matmul_row_parts(x_ref, w_ref, parts, store(_silu)))
    pl.when(j < silu_from)(lambda: _matmul_row_parts(x_ref, w_ref, parts, store(lambda acc: acc)))


def _proj(xb, wb, col0, ncols, silu_from, out_dtype, *, tm, tn, parts, name):
    m, d = xb.shape
    off = col0 // tn
    return pl.pallas_call(
        functools.partial(_proj_kernel, silu_from=silu_from, parts=parts),
        grid=(m // tm, ncols // tn),
        in_specs=[pl.BlockSpec((tm, d), lambda i, j: (i, 0)),
                  pl.BlockSpec((d, tn), lambda i, j: (0, j + off))],
        out_specs=pl.BlockSpec((tm, tn), lambda i, j: (i, j)),
        out_shape=jax.ShapeDtypeStruct((m, ncols), out_dtype),
        compiler_params=_params("parallel", "arbitrary"),
        name=name,
    )(xb, wb)


def _attn_kernel(*refs, n_cast):
    q_ref, k_ref, kn_ref, vt_ref = refs[:4]
    w_refs = refs[4:4 + n_cast]
    o_ref = refs[4 + n_cast]
    wb_refs = refs[5 + n_cast:5 + 2 * n_cast]
    m_ref, l_ref, acc_ref, s0_ref = refs[5 + 2 * n_cast:]
    ki = pl.program_id(3)

    _cast_in_kernel(w_refs, wb_refs)

    def scores_t(keys, g):
        return lax.dot_general(keys, q_ref[:, g * HEAD_DIM:(g + 1) * HEAD_DIM], NT_DIMS,
                               preferred_element_type=F32)

    @pl.when(ki == 0)
    def _():
        m_ref[...] = jnp.full_like(m_ref, -1e30)
        l_ref[...] = jnp.zeros_like(l_ref)
        acc_ref[...] = jnp.zeros_like(acc_ref)
        s0_ref[...] = scores_t(k_ref[...], 0)

    k = k_ref[...]
    vt = vt_ref[...]

    st_next = s0_ref[...]
    for g in range(Q_PER_KV):
        row = slice(g, g + 1)
        st = st_next
        if g + 1 < Q_PER_KV:
            st_next = scores_t(k, g + 1)
        else:
            s0_ref[...] = scores_t(kn_ref[...], 0)
        m_old = m_ref[row, :]
        m_new = jnp.maximum(m_old, jnp.max(st, axis=0, keepdims=True))
        alpha = jnp.exp2(m_old - m_new)
        p = jnp.exp2(st - m_new)
        l_ref[row, :] = alpha * l_ref[row, :] + jnp.sum(p, axis=0, keepdims=True)
        acc_ref[g] = alpha * acc_ref[g] + _dot(vt, p.astype(BF16))
        m_ref[row, :] = m_new

    @pl.when(ki == pl.num_programs(3) - 1)
    def _():
        for g in range(Q_PER_KV):
            o = acc_ref[g] / l_ref[g:g + 1, :]
            o_ref[:, g * HEAD_DIM:(g + 1) * HEAD_DIM] = o.T.astype(o_ref.dtype)


def _attention(qkv, vt, weights, batch, seq, *, tq, tk):
    m = qkv.shape[0]
    gw = Q_PER_KV * HEAD_DIM
    nq, nk = seq // tq, seq // tk
    k_col0 = N_Q_HEADS
    step = lambda b, h, qi, ki: ((b * N_KV_HEADS + h) * nq + qi) * nk + ki
    flat, slabs, rides = _cast_slabs(weights, batch * N_KV_HEADS * nq * nk, step)
    out = pl.pallas_call(
        functools.partial(_attn_kernel, n_cast=len(flat)),
        grid=(batch, N_KV_HEADS, nq, nk),
        in_specs=[pl.BlockSpec((tq, gw), lambda b, h, qi, ki: (b * nq + qi, h)),
                  pl.BlockSpec((tk, HEAD_DIM), lambda b, h, qi, ki: (b * nk + ki, k_col0 + h)),
                  pl.BlockSpec((tk, HEAD_DIM),
                               lambda b, h, qi, ki: (b * nk + jnp.minimum(ki + 1, nk - 1), k_col0 + h)),
                  pl.BlockSpec((HEAD_DIM, tk), lambda b, h, qi, ki: (b * N_KV_HEADS + h, ki))] + slabs,
        out_specs=[pl.BlockSpec((tq, gw), lambda b, h, qi, ki: (b * nq + qi, h))] + slabs,
        out_shape=[jax.ShapeDtypeStruct((m, N_Q_HEADS * HEAD_DIM), BF16)]
        + [jax.ShapeDtypeStruct(f.shape, BF16) for f in flat],
        scratch_shapes=[pltpu.VMEM((SUBLANES, tq), F32), pltpu.VMEM((SUBLANES, tq), F32),
                        pltpu.VMEM((Q_PER_KV, HEAD_DIM, tq), F32), pltpu.VMEM((tk, tq), F32)],
        compiler_params=_params("parallel", "parallel", "parallel", "arbitrary"),
        name="gqa_attention",
    )(qkv, qkv, qkv, vt, *flat)
    return out[0], _cast_results(weights, rides, out[1:])


def _hgrn_kernel(*refs, reverse, final, nc, hb):
    if final:
        q_ref, z_ref, v_ref, lb_ref, tri_ref, prev_ref, g_ref, gain_ref, o_ref, st_ref = refs
    else:
        q_ref, z_ref, v_ref, lb_ref, tri_ref, o_ref, st_ref = refs
    C, D = HG_CHUNK, HG_DIM
    T, W = nc * C, hb * D

    @pl.when(pl.program_id(2) == 0)
    def _():
        st_ref[...] = jnp.zeros_like(st_ref)

    lb = lb_ref[...]
    half_span = 0.5 * (1.0 - lb)
    g = half_span * jnp.tanh(0.5 * z_ref[...])
    key = half_span - g
    logf = jnp.log2(0.5 * (1.0 + lb) + g)

    hi = logf.astype(BF16)
    r1 = logf - hi.astype(F32)
    mid = r1.astype(BF16)
    lo = (r1 - mid.astype(F32)).astype(BF16)
    cat = jnp.concatenate([hi, mid, lo], axis=1)
    tri = tri_ref[...]
    parts = []
    for s in range(T // TRI_BLOCK):
        bc = _dot(tri, cat[s * TRI_BLOCK:(s + 1) * TRI_BLOCK, :])
        parts.append((bc[:, :W] + bc[:, W:2 * W]) + bc[:, 2 * W:])
    b = parts[0] if len(parts) == 1 else jnp.concatenate(parts, axis=0)
    b3 = b.reshape(nc, C, W)
    ref_row = C // 2 if reverse else C // 2 - 1
    end_row = 0 if reverse else C - 1
    bref = b3[:, ref_row:ref_row + 1, :]
    btot = b3[:, end_row:end_row + 1, :]

    q3 = q_ref[...].astype(F32).reshape(nc, C, W)
    q_in = q3 * jnp.exp2(b3 - bref)
    k_in = key.reshape(nc, C, W) * jnp.exp2(bref - b3)
    q_st = (q_in * jnp.exp2(bref)).astype(BF16).reshape(T, W)
    k_up = (k_in * jnp.exp2(btot - bref)).astype(BF16).reshape(T, W)
    dec = jnp.exp2(btot)
    q_in = q_in.astype(BF16).reshape(T, W)
    k_in = k_in.astype(BF16).reshape(T, W)

    P = 2 * C
    rowi = lax.broadcasted_iota(jnp.int32, (P, P), 0)
    coli = lax.broadcasted_iota(jnp.int32, (P, P), 1)
    mask = ((rowi >= C) == (coli >= C)) & ((coli >= rowi) if reverse else (coli <= rowi))
    order = range(nc - 1, -1, -1) if reverse else range(nc)
    for h in range(hb):
        sl = slice(h * D, (h + 1) * D)
        scores = []
        for pr in range(nc // 2):
            rows = slice(pr * P, (pr + 1) * P)
            a = lax.dot_general(q_in[rows, sl], k_in[rows, sl], NT_DIMS, preferred_element_type=F32)
            scores.append(jnp.where(mask, a, 0.0).astype(BF16))
        upd_t = [lax.dot_general(v_ref[c * C:(c + 1) * C, sl], k_up[c * C:(c + 1) * C, sl], TN_DIMS,
                                 preferred_element_type=F32) for c in range(nc)]
        state_t = st_ref[h]
        entering = [None] * nc
        for c in order:
            entering[c] = state_t.astype(BF16)
            state_t = state_t * dec[c, :, sl] + upd_t[c]
        st_ref[h] = state_t
        for pr in range(nc // 2):
            rows = slice(pr * P, (pr + 1) * P)
            o_pair = _dot(scores[pr], v_ref[rows, sl])
            for half in range(2):
                c = 2 * pr + half
                rows_c = slice(c * C, (c + 1) * C)
                o = o_pair[half * C:(half + 1) * C] + lax.dot_general(
                    q_st[rows_c, sl], entering[c], NT_DIMS, preferred_element_type=F32)
                if final:
                    tot = o + prev_ref[rows_c, sl]
                    ms = jnp.mean(tot * tot, axis=-1, keepdims=True)
                    y = tot * lax.rsqrt(ms + RMS_EPS) * gain_ref[:, sl] * g_ref[rows_c, sl].astype(F32)
                    o_ref[rows_c, sl] = y.astype(o_ref.dtype)
                else:
                    o_ref[rows_c, sl] = o


def _hgrn_pass(qh, z, vg, lb, tri, batch, seq, *, reverse, prev=None, gain=None, nc, hb):
    m, hv = qh.shape
    T, W = nc * HG_CHUNK, hb * HG_DIM
    nt, nh = seq // T, hv // W
    final = prev is not None

    def tmap(b, h, n):
        return b * nt + ((nt - 1 - n) if reverse else n)

    zoff = nh if reverse else 0
    in_specs = [pl.BlockSpec((T, W), lambda b, h, n: (tmap(b, h, n), h)),
                pl.BlockSpec((T, W), lambda b, h, n: (tmap(b, h, n), h + zoff)),
                pl.BlockSpec((T, W), lambda b, h, n: (tmap(b, h, n), h)),
                pl.BlockSpec((1, W), lambda b, h, n: (0, h)),
                pl.BlockSpec((TRI_BLOCK, TRI_BLOCK), lambda b, h, n: (0, 0))]
    args = [qh, z, vg, lb, tri]
    if final:
        in_specs += [pl.BlockSpec((T, W), lambda b, h, n: (tmap(b, h, n), h)),
                     pl.BlockSpec((T, W), lambda b, h, n: (tmap(b, h, n), h + nh)),
                     pl.BlockSpec((1, W), lambda b, h, n: (0, h))]
        args += [prev, vg, gain]
    return pl.pallas_call(
        functools.partial(_hgrn_kernel, reverse=reverse, final=final, nc=nc, hb=hb),
        grid=(batch, nh, nt),
        in_specs=in_specs,
        out_specs=pl.BlockSpec((T, W), lambda b, h, n: (tmap(b, h, n), h)),
        out_shape=jax.ShapeDtypeStruct((m, hv), BF16 if final else F32),
        scratch_shapes=[pltpu.VMEM((hb, HG_DIM, HG_DIM), F32)],
        compiler_params=_params("parallel", "parallel", "arbitrary"),
        name="hgrn2_bwd_merge" if final else "hgrn2_fwd",
    )(*args)


def _chunk_triangle(reverse):
    i = jnp.arange(TRI_BLOCK)
    same = (i[:, None] // HG_CHUNK) == (i[None, :] // HG_CHUNK)
    tri = (i[None, :] >= i[:, None]) if reverse else (i[None, :] <= i[:, None])
    return (same & tri).astype(BF16)


def _mix_kernel(x_ref, ya_ref, yh_ref, wga_ref, wgh_ref, wpa_ref, wpb_ref, bga_ref, bgh_ref, o_ref):
    x = x_ref[...]
    ga = _sigmoid(_dot(x, wga_ref[...]) + bga_ref[...])
    gh = _sigmoid(_dot(x, wgh_ref[...]) + bgh_ref[...])
    o = ga * _dot(ya_ref[...], wpa_ref[...]) + gh * _dot(yh_ref[...], wpb_ref[...])
    o_ref[...] = o.astype(o_ref.dtype)


def _mix(xb, ya, yh, w_gate_b, b_gate, w_pa_b, w_pb_b, *, tm, tn):
    m, d = xb.shape
    da, dh = ya.shape[1], yh.shape[1]
    nj = d // tn
    return pl.pallas_call(
        _mix_kernel,
        grid=(m // tm, nj),
        in_specs=[pl.BlockSpec((tm, d), lambda i, j: (i, 0)),
                  pl.BlockSpec((tm, da), lambda i, j: (i, 0)),
                  pl.BlockSpec((tm, dh), lambda i, j: (i, 0)),
                  pl.BlockSpec((d, tn), lambda i, j: (0, j)),
                  pl.BlockSpec((d, tn), lambda i, j: (0, j + nj)),
                  pl.BlockSpec((da, tn), lambda i, j: (0, j)),
                  pl.BlockSpec((dh, tn), lambda i, j: (0, j)),
                  pl.BlockSpec((1, tn), lambda i, j: (0, j)),
                  pl.BlockSpec((1, tn), lambda i, j: (0, j + nj))],
        out_specs=pl.BlockSpec((tm, tn), lambda i, j: (i, j)),
        out_shape=jax.ShapeDtypeStruct((m, d), BF16),
        compiler_params=_params("parallel", "arbitrary"),
        name="gated_branch_merge",
    )(xb, ya, yh, w_gate_b, w_gate_b, w_pa_b, w_pb_b, b_gate, b_gate)


def _mm_res_kernel(a_ref, w_ref, res_ref, o_ref, *, alpha):
    o_ref[...] = alpha * res_ref[...] + _dot(a_ref[...], w_ref[...])


def _mm_res(a, wb, res3, alpha, *, tm, name):
    m, k = a.shape
    nj, _, tn = res3.shape
    return pl.pallas_call(
        functools.partial(_mm_res_kernel, alpha=alpha),
        grid=(m // tm, nj),
        in_specs=[pl.BlockSpec((tm, k), lambda i, j: (i, 0)),
                  pl.BlockSpec((k, tn), lambda i, j: (0, j)),
                  pl.BlockSpec((None, tm, tn), lambda i, j: (j, i, 0))],
        out_specs=pl.BlockSpec((tm, tn), lambda i, j: (i, j)),
        out_shape=jax.ShapeDtypeStruct((m, nj * tn), F32),
        compiler_params=_params("parallel", "arbitrary"),
        name=name,
    )(a, wb, res3)


def _layer_norm_blocks(load, store, g_ref, b_ref, rows, groups, rb):
    width = groups * LANES
    for r in range(rows // rb):
        rs = slice(r * rb, (r + 1) * rb)

        def row_sum(fn):
            acc = fn(load(rs, 0))
            for k in range(1, groups):
                acc = acc + fn(load(rs, k))
            return jnp.sum(acc, axis=-1, keepdims=True)

        mu = row_sum(lambda t: t) / width
        inv = lax.rsqrt(row_sum(lambda t: (t - mu) * (t - mu)) / width + LN_EPS)
        for k in range(groups):
            cols = slice(k * LANES, (k + 1) * LANES)
            store(rs, k, (load(rs, k) - mu) * inv * g_ref[:, cols] + b_ref[:, cols])


def _mm_res_ln_kernel(a_ref, w_ref, res_ref, g_ref, b_ref, o_ref, ob_ref, *, alpha):
    j = pl.program_id(1)
    nj, _, tn = o_ref.shape
    o_ref[j] = alpha * res_ref[...] + _dot(a_ref[...], w_ref[...])

    @pl.when(j == nj - 1)
    def _():
        per = tn // LANES
        at = lambda k: slice((k % per) * LANES, (k % per + 1) * LANES)

        def store(rs, k, y):
            o_ref[k // per, rs, at(k)] = y
            ob_ref[rs, k * LANES:(k + 1) * LANES] = y.astype(ob_ref.dtype)

        _layer_norm_blocks(lambda rs, k: o_ref[k // per, rs, at(k)], store, g_ref, b_ref,
                           o_ref.shape[1], nj * per, LN_ROW_BLOCK)


def _mm_res_ln(a, wb, res, alpha, g, b, *, tm, tn, name):
    m, k = a.shape
    n = wb.shape[1]
    nj = n // tn
    vec = pl.BlockSpec((1, n), lambda i, j: (0, 0))
    return pl.pallas_call(
        functools.partial(_mm_res_ln_kernel, alpha=alpha),
        grid=(m // tm, nj),
        in_specs=[pl.BlockSpec((tm, k), lambda i, j: (i, 0)),
                  pl.BlockSpec((k, tn), lambda i, j: (0, j)),
                  pl.BlockSpec((tm, tn), lambda i, j: (i, j)),
                  vec, vec],
        out_specs=[pl.BlockSpec((nj, tm, tn), lambda i, j: (0, i, 0)),
                   pl.BlockSpec((tm, n), lambda i, j: (i, 0))],
        out_shape=[jax.ShapeDtypeStruct((nj, m, tn), F32), jax.ShapeDtypeStruct((m, n), BF16)],
        compiler_params=_params("parallel", "arbitrary"),
        name=name,
    )(a, wb, res, g, b)


def _ffn_up_kernel(*refs, blocks_per_seq, parts, n_cast):
    x_ref, xp_ref, xn_ref, wu_ref, wg_ref, cw_ref, cb_ref = refs[:7]
    o_ref = refs[7 + n_cast]
    xh_ref, g_ref = refs[8 + 2 * n_cast:]
    _cast_in_kernel(refs[7:7 + n_cast], refs[8 + n_cast:8 + 2 * n_cast])
    i, j = pl.program_id(0), pl.program_id(1)
    tm = x_ref.shape[0]
    halo = xp_ref.shape[0]

    @pl.when(j == 0)
    def _():
        first = (i % blocks_per_seq) == 0
        last = (i % blocks_per_seq) == blocks_per_seq - 1
        xh_ref[0:halo] = jnp.where(first, jnp.zeros_like(xp_ref), xp_ref[...])
        xh_ref[halo:halo + tm] = x_ref[...]
        xh_ref[halo + tm:] = jnp.where(last, jnp.zeros_like(xn_ref), xn_ref[...])

    rows = tm // parts
    w = cw_ref[...]

    def matmuls(s):
        lo = 0 if s == 0 else halo + s * rows
        hi = tm + 2 * halo if s == parts - 1 else halo + (s + 1) * rows
        g_ref[lo:hi] = _dot(xh_ref[lo:hi], wg_ref[...])
        return _dot(xh_ref[halo + s * rows:halo + (s + 1) * rows], wu_ref[...])

    def gate(s, u):
        pad = SUBLANES
        win = g_ref[halo + s * rows - pad:halo + (s + 1) * rows + pad]
        own = slice(pad, pad + rows)
        g_prev = pltpu.roll(win, 1, 0)[own]
        g_next = pltpu.roll(win, rows + 2 * pad - 1, 0)[own]
        gc = g_prev * w[0:1] + win[own] * w[1:2] + g_next * w[2:3] + cb_ref[...]
        o_ref[s * rows:(s + 1) * rows] = (_silu(gc) * u).astype(o_ref.dtype)

    u_prev = matmuls(0)
    for s in range(1, parts):
        u_next = matmuls(s)
        gate(s - 1, u_prev)
        u_prev = u_next
    gate(parts - 1, u_prev)


def _ffn_up(xb, w_up_b, conv_w, conv_b, weights, seq, *, tm, tn, parts):
    m, d = xb.shape
    dff = w_up_b.shape[1] // 2
    nj = dff // tn
    halo = BF16_ROWS
    rb = tm // halo
    last_halo = m // halo - 1
    flat, slabs, rides = _cast_slabs(weights, (m // tm) * nj, lambda i, j: i * nj + j)
    out = pl.pallas_call(
        functools.partial(_ffn_up_kernel, blocks_per_seq=seq // tm, parts=parts, n_cast=len(flat)),
        grid=(m // tm, nj),
        in_specs=[pl.BlockSpec((tm, d), lambda i, j: (i, 0), pipeline_mode=pl.Buffered(1)),
                  pl.BlockSpec((halo, d), lambda i, j: (jnp.maximum(i * rb - 1, 0), 0)),
                  pl.BlockSpec((halo, d), lambda i, j: (jnp.minimum((i + 1) * rb, last_halo), 0)),
                  pl.BlockSpec((d, tn), lambda i, j: (0, j)),
                  pl.BlockSpec((d, tn), lambda i, j: (0, j + nj)),
                  pl.BlockSpec((3, tn), lambda i, j: (0, j)),
                  pl.BlockSpec((1, tn), lambda i, j: (0, j))] + slabs,
        out_specs=[pl.BlockSpec((tm, tn), lambda i, j: (i, j))] + slabs,
        out_shape=[jax.ShapeDtypeStruct((m, dff), BF16)] + [jax.ShapeDtypeStruct(f.shape, BF16) for f in flat],
        scratch_shapes=[pltpu.VMEM((tm + 2 * halo, d), BF16),
                        pltpu.VMEM((tm + 2 * halo, tn), F32)],
        compiler_params=_params("parallel", "arbitrary"),
        name="ffn_up_conv_gate",
    )(xb, xb, xb, w_up_b, w_up_b, conv_w, conv_b, *flat)
    return out[0], _cast_results(weights, rides, out[1:])


def _ln_ple_kernel(z_ref, g_ref, b_ref, pb_ref, wpg_ref, wple_ref, o_ref, x_ref, xb_ref):
    j = pl.program_id(1)
    nj, _, tn = x_ref.shape

    @pl.when(j == 0)
    def _():
        per = tn // LANES

        def store(rs, k, y):
            x_ref[k // per, rs, (k % per) * LANES:(k % per + 1) * LANES] = y
            xb_ref[rs, k * LANES:(k + 1) * LANES] = y.astype(xb_ref.dtype)

        _layer_norm_blocks(lambda rs, k: z_ref[rs, k * LANES:(k + 1) * LANES], store, g_ref, b_ref,
                           z_ref.shape[0], nj * per, LN_ROW_BLOCK)

    gate = _sigmoid(_dot(xb_ref[...], wpg_ref[...]))
    o_ref[...] = x_ref[j] + gate * _dot(pb_ref[...], wple_ref[...])


def _ln_ple(z, g, b, pb, w_pg_b, w_ple_b, *, tm, tn):
    m, d = z.shape
    dp = pb.shape[1]
    vec = pl.BlockSpec((1, d), lambda i, j: (0, 0))
    return pl.pallas_call(
        _ln_ple_kernel,
        grid=(m // tm, d // tn),
        in_specs=[pl.BlockSpec((tm, d), lambda i, j: (i, 0)),
                  vec, vec,
                  pl.BlockSpec((tm, dp), lambda i, j: (i, 0)),
                  pl.BlockSpec((d, tn), lambda i, j: (0, j)),
                  pl.BlockSpec((dp, tn), lambda i, j: (0, j))],
        out_specs=pl.BlockSpec((tm, tn), lambda i, j: (i, j)),
        out_shape=jax.ShapeDtypeStruct((m, d), F32),
        scratch_shapes=[pltpu.VMEM((d // tn, tm, tn), F32), pltpu.VMEM((tm, d), BF16)],
        compiler_params=_params("parallel", "arbitrary"),
        name="ln_ple_gate",
    )(z, g, b, pb, w_pg_b, w_ple_b)


def _rope_tables(seq):
    pos = jnp.arange(seq)
    row = (pos // GRID_W).astype(F32)
    col = (pos % GRID_W).astype(F32)
    sec = HEAD_DIM // 2
    inv = ROPE_THETA ** (-jnp.arange(0, sec, 2, dtype=F32) / sec)
    ang_r = row[:, None] * inv[None, :]
    ang_c = col[:, None] * inv[None, :]
    ang = jnp.concatenate([ang_r, ang_r, ang_c, ang_c], axis=-1)
    cos, sin = jnp.cos(ang), jnp.sin(ang)
    first_half = (jnp.arange(HEAD_DIM) % sec) < sec // 2
    sa = jnp.where(first_half[None, :], -sin, 0.0)
    sb = jnp.where(first_half[None, :], 0.0, sin)
    return cos, sa, sb


def _qk_rotary_tables(rope, q_gain, k_gain):
    cos, sa, sb = rope
    quarter = HEAD_DIM // 4

    def fold(gain, scale):
        g = gain.astype(F32) * scale
        return (cos * g[None, :], sa * jnp.roll(g, HEAD_DIM - quarter)[None, :], sb * jnp.roll(g, quarter)[None, :])

    tq = fold(q_gain, HEAD_DIM ** -0.5 * LOG2_E)
    tk = fold(k_gain, 1.0)
    return tuple(jnp.stack([a, b]) for a, b in zip(tq, tk))


def _layer(x, p, w_in, q_norm, k_norm, lb_f, lb_b, hg_norm, w_pa, w_pb, w_gate, b_gate, w_o,
           ln1_g, ln1_b, w_up, conv_w, conv_b, w_down, ln2_g, ln2_b, w_pg, w_ple,
           batch, seq, alpha, tables, tris):
    m, d = x.shape
    attn_dim = N_Q_HEADS * HEAD_DIM
    kv_dim = N_KV_HEADS * HEAD_DIM
    hq = HG_HEADS * HG_DIM
    row2 = lambda v: v.reshape(1, -1).astype(F32)
    bf = lambda v: v.astype(BF16)

    xb = bf(x)
    tm = min(1024, seq)

    qkv, w_hg_b = _qkv_proj(xb, w_in, _qk_rotary_tables(tables, q_norm, k_norm), seq, tm=tm, tn=512, parts=2)
    qh = _proj(xb, w_hg_b, 0, hq, 0, BF16, tm=tm, tn=512, parts=2, name="hgrn_q_proj")
    z = _proj(xb, w_hg_b, hq, 2 * hq, 2 * hq // 512, F32, tm=tm, tn=512, parts=2, name="hgrn_gate_proj")
    vg = _proj(xb, w_hg_b, 3 * hq, 2 * hq, hq // 512, BF16, tm=tm, tn=512, parts=2, name="hgrn_vg_proj")

    v = qkv[:, attn_dim + kv_dim:]
    vt = v.reshape(batch, seq, kv_dim).transpose(0, 2, 1).reshape(batch * kv_dim, seq)
    later_weights = [w_gate, w_pa, w_pb, w_o, w_up, w_pg, w_ple]
    y_attn, (w_gate_b, w_pa_b, w_pb_b, w_o_b, w_up_b, w_pg_b, w_ple_b) = _attention(
        qkv, vt, later_weights, batch, seq, tq=min(1024, seq), tk=min(2048, seq))

    nc = min(16, seq // HG_CHUNK)
    o_fwd = _hgrn_pass(qh, z, vg, row2(lb_f), tris[0], batch, seq, reverse=False, nc=nc, hb=4)
    y_hgrn = _hgrn_pass(qh, z, vg, row2(lb_b), tris[1], batch, seq, reverse=True,
                        prev=o_fwd, gain=row2(hg_norm), nc=nc, hb=4)

    mixed = _mix(xb, y_attn, y_hgrn, w_gate_b, row2(b_gate), w_pa_b, w_pb_b, tm=min(512, seq), tn=512)
    x1_blocked, x1b = _mm_res_ln(mixed, w_o_b, x, alpha, row2(ln1_g), row2(ln1_b),
                                 tm=min(512, seq), tn=512, name="attn_out_residual_ln")

    act, (w_down_b,) = _ffn_up(x1b, w_up_b, conv_w.astype(F32), row2(conv_b), [w_down], seq,
                               tm=min(2048, seq), tn=256, parts=4)
    z2 = _mm_res(act, w_down_b, x1_blocked, alpha, tm=min(512, seq), name="ffn_down_residual")

    return _ln_ple(z2, row2(ln2_g), row2(ln2_b), bf(p), w_pg_b, w_ple_b, tm=min(512, seq), tn=512)


def kernel(x, p, w_in, q_norm, k_norm, lb_logits, hg_norm, w_pa, w_pb, w_gate, b_gate, w_o, ln1_g, ln1_b,
           w_up, conv_w, conv_b, w_down, ln2_g, ln2_b, w_pg, w_ple):
    batch, seq, d = x.shape
    depth = w_in.shape[0]
    alpha = (2.0 * depth) ** 0.25
    tables = _rope_tables(seq)
    tris = (_chunk_triangle(False), _chunk_triangle(True))
    lb_all = jnp.cumsum(jax.nn.softmax(lb_logits.astype(F32), axis=1), axis=1)
    h = x.reshape(batch * seq, d)
    for i in range(depth):
        h = _layer(h, p[i].reshape(batch * seq, -1), w_in[i], q_norm[i], k_norm[i], lb_all[0, i], lb_all[1, i],
                   hg_norm[i], w_pa[i], w_pb[i], w_gate[i], b_gate[i], w_o[i], ln1_g[i], ln1_b[i],
                   w_up[i], conv_w[i], conv_b[i], w_down[i], ln2_g[i], ln2_b[i], w_pg[i], w_ple[i],
                   batch, seq, alpha, tables, tris)
    return h.reshape(batch, seq, d)
```

```python
import functools

import jax
import jax.numpy as jnp
from jax import lax
from jax.experimental import pallas as pl
from jax.experimental.pallas import tpu as pltpu

F32 = jnp.float32
BF16 = jnp.bfloat16

GRID_W = 64
HEAD_DIM = 128
N_Q_HEADS = 16
N_KV_HEADS = 4
Q_PER_KV = N_Q_HEADS // N_KV_HEADS
ROPE_THETA = 10000.0
HG_HEADS = 16
HG_DIM = 128
HG_CHUNK = 64
RMS_EPS = 1e-6
LN_EPS = 1e-5

LOG2_E = 1.4426950408889634
LANES = 128
SUBLANES = 8
BF16_ROWS = 16
TRI_BLOCK = 256
LN_ROW_BLOCK = 64
VMEM_LIMIT = 56 * 1024 * 1024

NT_DIMS = (((1,), (1,)), ((), ()))
TN_DIMS = (((0,), (0,)), ((), ()))


def _params(*sem):
    return pltpu.CompilerParams(dimension_semantics=sem, vmem_limit_bytes=VMEM_LIMIT)


def _dot(a, b):
    return jnp.dot(a, b, preferred_element_type=F32)


def _sigmoid(v):
    return 0.5 + 0.5 * jnp.tanh(0.5 * v)


def _silu(v):
    h = 0.5 * v
    return h + h * jnp.tanh(h)


def _cast_slabs(weights, steps, step_index):
    tile = BF16_ROWS

    def spec(w):
        ntiles = w.shape[0] // tile
        if w.shape[0] % tile or (ntiles % steps and steps % ntiles):
            return None
        per_step, hold = max(ntiles // steps, 1), max(steps // ntiles, 1)
        return pl.BlockSpec((tile * per_step, w.shape[1]), lambda *ids: (step_index(*ids) // hold, 0))

    specs = [spec(w) for w in weights]
    rides = [s is not None for s in specs]
    return [w for w, r in zip(weights, rides) if r], [s for s in specs if s is not None], rides


def _cast_in_kernel(w_refs, wb_refs):
    for w_ref, wb_ref in zip(w_refs, wb_refs):
        wb_ref[...] = w_ref[...].astype(wb_ref.dtype)


def _cast_results(weights, rides, cast_outputs):
    cast = iter(cast_outputs)
    return [next(cast) if r else w.astype(BF16) for w, r in zip(weights, rides)]


def _matmul_row_parts(x_ref, w_ref, parts, epilogue):
    rows = x_ref.shape[0] // parts
    part = lambda s: slice(s * rows, (s + 1) * rows)
    acc_prev = _dot(x_ref[part(0)], w_ref[...])
    for s in range(1, parts):
        acc = _dot(x_ref[part(s)], w_ref[...])
        epilogue(part(s - 1), acc_prev)
        acc_prev = acc
    epilogue(part(parts - 1), acc_prev)


def _qkv_kernel(x_ref, w_ref, c0_ref, c1_ref, c2_ref, o_ref, *, nqb, nkb, parts):
    j = pl.program_id(1)
    quarter = HEAD_DIM // 4

    def norm_rope(rs, acc):
        c0, c1, c2 = c0_ref[rs], c1_ref[rs], c2_ref[rs]
        for h in range(acc.shape[1] // HEAD_DIM):
            sl = slice(h * HEAD_DIM, (h + 1) * HEAD_DIM)
            blk = acc[:, sl]
            inv = lax.rsqrt(jnp.mean(blk * blk, axis=-1, keepdims=True) + RMS_EPS)
            rot = blk * c0 + pltpu.roll(blk, HEAD_DIM - quarter, 1) * c1 + pltpu.roll(blk, quarter, 1) * c2
            o_ref[rs, sl] = (rot * inv).astype(o_ref.dtype)

    def passthrough(rs, acc):
        o_ref[rs] = acc.astype(o_ref.dtype)

    pl.when(j < nqb + nkb)(lambda: _matmul_row_parts(x_ref, w_ref, parts, norm_rope))
    pl.when(j >= nqb + nkb)(lambda: _matmul_row_parts(x_ref, w_ref, parts, passthrough))


def _qkv_proj(xb, w_in_b, tables, seq, *, tm, tn, parts):
    m, d = xb.shape
    attn_dim = N_Q_HEADS * HEAD_DIM
    kv_dim = N_KV_HEADS * HEAD_DIM
    n = attn_dim + 2 * kv_dim
    nsb = seq // tm
    nqb = attn_dim // tn
    tab = pl.BlockSpec((None, tm, HEAD_DIM), lambda i, j: ((j >= nqb).astype(jnp.int32), i % nsb, 0))
    return pl.pallas_call(
        functools.partial(_qkv_kernel, nqb=nqb, nkb=kv_dim // tn, parts=parts),
        grid=(m // tm, n // tn),
        in_specs=[pl.BlockSpec((tm, d), lambda i, j: (i, 0)),
                  pl.BlockSpec((d, tn), lambda i, j: (0, j)),
                  tab, tab, tab],
        out_specs=pl.BlockSpec((tm, tn), lambda i, j: (i, j)),
        out_shape=jax.ShapeDtypeStruct((m, n), BF16),
        compiler_params=_params("parallel", "arbitrary"),
        name="qkv_proj",
    )(xb, w_in_b, *tables)


def _proj_kernel(x_ref, w_ref, o_ref, *, silu_from, parts):
    j = pl.program_id(1)

    def store(fn):
        def epilogue(rs, acc):
            o_ref[rs] = fn(acc).astype(o_ref.dtype)
        return epilogue

    pl.when(j >= silu_from)(lambda: _matmul_row_parts(x_ref, w_ref, parts, store(_silu)))
    pl.when(j < silu_from)(lambda: _matmul_row_parts(x_ref, w_ref, parts, store(lambda acc: acc)))


def _proj(xb, wb, col0, ncols, silu_from, out_dtype, *, tm, tn, parts, name):
    m, d = xb.shape
    off = col0 // tn
    return pl.pallas_call(
        functools.partial(_proj_kernel, silu_from=silu_from, parts=parts),
        grid=(m // tm, ncols // tn),
        in_specs=[pl.BlockSpec((tm, d), lambda i, j: (i, 0)),
                  pl.BlockSpec((d, tn), lambda i, j: (0, j + off))],
        out_specs=pl.BlockSpec((tm, tn), lambda i, j: (i, j)),
        out_shape=jax.ShapeDtypeStruct((m, ncols), out_dtype),
        compiler_params=_params("parallel", "arbitrary"),
        name=name,
    )(xb, wb)


def _attn_kernel(*refs, n_cast):
    q_ref, k_ref, kn_ref, vt_ref = refs[:4]
    w_refs = refs[4:4 + n_cast]
    o_ref = refs[4 + n_cast]
    wb_refs = refs[5 + n_cast:5 + 2 * n_cast]
    m_ref, l_ref, acc_ref, s0_ref = refs[5 + 2 * n_cast:]
    ki = pl.program_id(3)

    _cast_in_kernel(w_refs, wb_refs)

    def scores_t(keys, g):
        return lax.dot_general(keys, q_ref[:, g * HEAD_DIM:(g + 1) * HEAD_DIM], NT_DIMS,
                               preferred_element_type=F32)

    @pl.when(ki == 0)
    def _():
        m_ref[...] = jnp.full_like(m_ref, -1e30)
        l_ref[...] = jnp.zeros_like(l_ref)
        acc_ref[...] = jnp.zeros_like(acc_ref)
        s0_ref[...] = scores_t(k_ref[...], 0)

    k = k_ref[...]
    vt = vt_ref[...]

    st_next = s0_ref[...]
    for g in range(Q_PER_KV):
        row = slice(g, g + 1)
        st = st_next
        if g + 1 < Q_PER_KV:
            st_next = scores_t(k, g + 1)
        else:
            s0_ref[...] = scores_t(kn_ref[...], 0)
        m_old = m_ref[row, :]
        m_new = jnp.maximum(m_old, jnp.max(st, axis=0, keepdims=True))
        alpha = jnp.exp2(m_old - m_new)
        p = jnp.exp2(st - m_new)
        l_ref[row, :] = alpha * l_ref[row, :] + jnp.sum(p, axis=0, keepdims=True)
        acc_ref[g] = alpha * acc_ref[g] + _dot(vt, p.astype(BF16))
        m_ref[row, :] = m_new

    @pl.when(ki == pl.num_programs(3) - 1)
    def _():
        for g in range(Q_PER_KV):
            o = acc_ref[g] / l_ref[g:g + 1, :]
            o_ref[:, g * HEAD_DIM:(g + 1) * HEAD_DIM] = o.T.astype(o_ref.dtype)


def _attention(qkv, vt, weights, batch, seq, *, tq, tk):
    m = qkv.shape[0]
    gw = Q_PER_KV * HEAD_DIM
    nq, nk = seq // tq, seq // tk
    k_col0 = N_Q_HEADS
    step = lambda b, h, qi, ki: ((b * N_KV_HEADS + h) * nq + qi) * nk + ki
    flat, slabs, rides = _cast_slabs(weights, batch * N_KV_HEADS * nq * nk, step)
    out = pl.pallas_call(
        functools.partial(_attn_kernel, n_cast=len(flat)),
        grid=(batch, N_KV_HEADS, nq, nk),
        in_specs=[pl.BlockSpec((tq, gw), lambda b, h, qi, ki: (b * nq + qi, h)),
                  pl.BlockSpec((tk, HEAD_DIM), lambda b, h, qi, ki: (b * nk + ki, k_col0 + h)),
                  pl.BlockSpec((tk, HEAD_DIM),
                               lambda b, h, qi, ki: (b * nk + jnp.minimum(ki + 1, nk - 1), k_col0 + h)),
                  pl.BlockSpec((HEAD_DIM, tk), lambda b, h, qi, ki: (b * N_KV_HEADS + h, ki))] + slabs,
        out_specs=[pl.BlockSpec((tq, gw), lambda b, h, qi, ki: (b * nq + qi, h))] + slabs,
        out_shape=[jax.ShapeDtypeStruct((m, N_Q_HEADS * HEAD_DIM), BF16)]
        + [jax.ShapeDtypeStruct(f.shape, BF16) for f in flat],
        scratch_shapes=[pltpu.VMEM((SUBLANES, tq), F32), pltpu.VMEM((SUBLANES, tq), F32),
                        pltpu.VMEM((Q_PER_KV, HEAD_DIM, tq), F32), pltpu.VMEM((tk, tq), F32)],
        compiler_params=_params("parallel", "parallel", "parallel", "arbitrary"),
        name="gqa_attention",
    )(qkv, qkv, qkv, vt, *flat)
    return out[0], _cast_results(weights, rides, out[1:])


def _hgrn_kernel(*refs, reverse, final, nc, hb):
    if final:
        q_ref, z_ref, v_ref, lb_ref, tri_ref, prev_ref, g_ref, gain_ref, o_ref, st_ref = refs
    else:
        q_ref, z_ref, v_ref, lb_ref, tri_ref, o_ref, st_ref = refs
    C, D = HG_CHUNK, HG_DIM
    T, W = nc * C, hb * D

    @pl.when(pl.program_id(2) == 0)
    def _():
        st_ref[...] = jnp.zeros_like(st_ref)

    lb = lb_ref[...]
    half_span = 0.5 * (1.0 - lb)
    g = half_span * jnp.tanh(0.5 * z_ref[...])
    key = half_span - g
    logf = jnp.log2(0.5 * (1.0 + lb) + g)

    hi = logf.astype(BF16)
    r1 = logf - hi.astype(F32)
    mid = r1.astype(BF16)
    lo = (r1 - mid.astype(F32)).astype(BF16)
    cat = jnp.concatenate([hi, mid, lo], axis=1)
    tri = tri_ref[...]
    parts = []
    for s in range(T // TRI_BLOCK):
        bc = _dot(tri, cat[s * TRI_BLOCK:(s + 1) * TRI_BLOCK, :])
        parts.append((bc[:, :W] + bc[:, W:2 * W]) + bc[:, 2 * W:])
    b = parts[0] if len(parts) == 1 else jnp.concatenate(parts, axis=0)
    b3 = b.reshape(nc, C, W)
    ref_row = C // 2 if reverse else C // 2 - 1
    end_row = 0 if reverse else C - 1
    bref = b3[:, ref_row:ref_row + 1, :]
    btot = b3[:, end_row:end_row + 1, :]

    q3 = q_ref[...].astype(F32).reshape(nc, C, W)
    q_in = q3 * jnp.exp2(b3 - bref)
    k_in = key.reshape(nc, C, W) * jnp.exp2(bref - b3)
    q_st = (q_in * jnp.exp2(bref)).astype(BF16).reshape(T, W)
    k_up = (k_in * jnp.exp2(btot - bref)).astype(BF16).reshape(T, W)
    dec = jnp.exp2(btot)
    q_in = q_in.astype(BF16).reshape(T, W)
    k_in = k_in.astype(BF16).reshape(T, W)

    P = 2 * C
    rowi = lax.broadcasted_iota(jnp.int32, (P, P), 0)
    coli = lax.broadcasted_iota(jnp.int32, (P, P), 1)
    mask = ((rowi >= C) == (coli >= C)) & ((coli >= rowi) if reverse else (coli <= rowi))
    order = range(nc - 1, -1, -1) if reverse else range(nc)
    for h in range(hb):
        sl = slice(h * D, (h + 1) * D)
        scores = []
        for pr in range(nc // 2):
            rows = slice(pr * P, (pr + 1) * P)
            a = lax.dot_general(q_in[rows, sl], k_in[rows, sl], NT_DIMS, preferred_element_type=F32)
            scores.append(jnp.where(mask, a, 0.0).astype(BF16))
        upd_t = [lax.dot_general(v_ref[c * C:(c + 1) * C, sl], k_up[c * C:(c + 1) * C, sl], TN_DIMS,
                                 preferred_element_type=F32) for c in range(nc)]
        state_t = st_ref[h]
        entering = [None] * nc
        for c in order:
            entering[c] = state_t.astype(BF16)
            state_t = state_t * dec[c, :, sl] + upd_t[c]
        st_ref[h] = state_t
        for pr in range(nc // 2):
            rows = slice(pr * P, (pr + 1) * P)
            o_pair = _dot(scores[pr], v_ref[rows, sl])
            for half in range(2):
                c = 2 * pr + half
                rows_c = slice(c * C, (c + 1) * C)
                o = o_pair[half * C:(half + 1) * C] + lax.dot_general(
                    q_st[rows_c, sl], entering[c], NT_DIMS, preferred_element_type=F32)
                if final:
                    tot = o + prev_ref[rows_c, sl]
                    ms = jnp.mean(tot * tot, axis=-1, keepdims=True)
                    y = tot * lax.rsqrt(ms + RMS_EPS) * gain_ref[:, sl] * g_ref[rows_c, sl].astype(F32)
                    o_ref[rows_c, sl] = y.astype(o_ref.dtype)
                else:
                    o_ref[rows_c, sl] = o


def _hgrn_pass(qh, z, vg, lb, tri, batch, seq, *, reverse, prev=None, gain=None, nc, hb):
    m, hv = qh.shape
    T, W = nc * HG_CHUNK, hb * HG_DIM
    nt, nh = seq // T, hv // W
    final = prev is not None

    def tmap(b, h, n):
        return b * nt + ((nt - 1 - n) if reverse else n)

    zoff = nh if reverse else 0
    in_specs = [pl.BlockSpec((T, W), lambda b, h, n: (tmap(b, h, n), h)),
                pl.BlockSpec((T, W), lambda b, h, n: (tmap(b, h, n), h + zoff)),
                pl.BlockSpec((T, W), lambda b, h, n: (tmap(b, h, n), h)),
                pl.BlockSpec((1, W), lambda b, h, n: (0, h)),
                pl.BlockSpec((TRI_BLOCK, TRI_BLOCK), lambda b, h, n: (0, 0))]
    args = [qh, z, vg, lb, tri]
    if final:
        in_specs += [pl.BlockSpec((T, W), lambda b, h, n: (tmap(b, h, n), h)),
                     pl.BlockSpec((T, W), lambda b, h, n: (tmap(b, h, n), h + nh)),
                     pl.BlockSpec((1, W), lambda b, h, n: (0, h))]
        args += [prev, vg, gain]
    return pl.pallas_call(
        functools.partial(_hgrn_kernel, reverse=reverse, final=final, nc=nc, hb=hb),
        grid=(batch, nh, nt),
        in_specs=in_specs,
        out_specs=pl.BlockSpec((T, W), lambda b, h, n: (tmap(b, h, n), h)),
        out_shape=jax.ShapeDtypeStruct((m, hv), BF16 if final else F32),
        scratch_shapes=[pltpu.VMEM((hb, HG_DIM, HG_DIM), F32)],
        compiler_params=_params("parallel", "parallel", "arbitrary"),
        name="hgrn2_bwd_merge" if final else "hgrn2_fwd",
    )(*args)


def _chunk_triangle(reverse):
    i = jnp.arange(TRI_BLOCK)
    same = (i[:, None] // HG_CHUNK) == (i[None, :] // HG_CHUNK)
    tri = (i[None, :] >= i[:, None]) if reverse else (i[None, :] <= i[:, None])
    return (same & tri).astype(BF16)


def _mix_kernel(x_ref, ya_ref, yh_ref, wga_ref, wgh_ref, wpa_ref, wpb_ref, bga_ref, bgh_ref, o_ref):
    x = x_ref[...]
    ga = _sigmoid(_dot(x, wga_ref[...]) + bga_ref[...])
    gh = _sigmoid(_dot(x, wgh_ref[...]) + bgh_ref[...])
    o = ga * _dot(ya_ref[...], wpa_ref[...]) + gh * _dot(yh_ref[...], wpb_ref[...])
    o_ref[...] = o.astype(o_ref.dtype)


def _mix(xb, ya, yh, w_gate_b, b_gate, w_pa_b, w_pb_b, *, tm, tn):
    m, d = xb.shape
    da, dh = ya.shape[1], yh.shape[1]
    nj = d // tn
    return pl.pallas_call(
        _mix_kernel,
        grid=(m // tm, nj),
        in_specs=[pl.BlockSpec((tm, d), lambda i, j: (i, 0)),
                  pl.BlockSpec((tm, da), lambda i, j: (i, 0)),
                  pl.BlockSpec((tm, dh), lambda i, j: (i, 0)),
                  pl.BlockSpec((d, tn), lambda i, j: (0, j)),
                  pl.BlockSpec((d, tn), lambda i, j: (0, j + nj)),
                  pl.BlockSpec((da, tn), lambda i, j: (0, j)),
                  pl.BlockSpec((dh, tn), lambda i, j: (0, j)),
                  pl.BlockSpec((1, tn), lambda i, j: (0, j)),
                  pl.BlockSpec((1, tn), lambda i, j: (0, j + nj))],
        out_specs=pl.BlockSpec((tm, tn), lambda i, j: (i, j)),
        out_shape=jax.ShapeDtypeStruct((m, d), BF16),
        compiler_params=_params("parallel", "arbitrary"),
        name="gated_branch_merge",
    )(xb, ya, yh, w_gate_b, w_gate_b, w_pa_b, w_pb_b, b_gate, b_gate)


def _mm_res_kernel(a_ref, w_ref, res_ref, o_ref, *, alpha):
    o_ref[...] = alpha * res_ref[...] + _dot(a_ref[...], w_ref[...])


def _mm_res(a, wb, res, alpha, *, tm, tn, name):
    m, k = a.shape
    n = wb.shape[1]
    return pl.pallas_call(
        functools.partial(_mm_res_kernel, alpha=alpha),
        grid=(m // tm, n // tn),
        in_specs=[pl.BlockSpec((tm, k), lambda i, j: (i, 0)),
                  pl.BlockSpec((k, tn), lambda i, j: (0, j)),
                  pl.BlockSpec((tm, tn), lambda i, j: (i, j))],
        out_specs=pl.BlockSpec((tm, tn), lambda i, j: (i, j)),
        out_shape=jax.ShapeDtypeStruct((m, n), F32),
        compiler_params=_params("parallel", "arbitrary"),
        name=name,
    )(a, wb, res)


def _layer_norm_blocks(load, store, g_ref, b_ref, rows, groups, rb):
    width = groups * LANES
    for r in range(rows // rb):
        rs = slice(r * rb, (r + 1) * rb)

        def row_sum(fn):
            acc = fn(load(rs, 0))
            for k in range(1, groups):
                acc = acc + fn(load(rs, k))
            return jnp.sum(acc, axis=-1, keepdims=True)

        mu = row_sum(lambda t: t) / width
        inv = lax.rsqrt(row_sum(lambda t: (t - mu) * (t - mu)) / width + LN_EPS)
        for k in range(groups):
            cols = slice(k * LANES, (k + 1) * LANES)
            store(rs, k, (load(rs, k) - mu) * inv * g_ref[:, cols] + b_ref[:, cols])


def _mm_res_ln_kernel(a_ref, w_ref, res_ref, g_ref, b_ref, o_ref, ob_ref, z_ref, mu_ref, inv_ref, *, alpha, parts):
    j = pl.program_id(1)
    nj, tm, tn = z_ref.shape
    per = tn // LANES
    rb = LN_ROW_BLOCK

    @pl.when(j < nj)
    def _():
        def residual(rs, acc):
            z_ref[j, rs] = alpha * res_ref[rs] + acc
        _matmul_row_parts(a_ref, w_ref, parts, residual)

    @pl.when(j == nj)
    def _():
        width = nj * tn
        for r in range(tm // rb):
            rs = slice(r * rb, (r + 1) * rb)

            def row_sum(fn):
                acc = None
                for k in range(nj * per):
                    t = fn(z_ref[k // per, rs, (k % per) * LANES:(k % per + 1) * LANES])
                    acc = t if acc is None else acc + t
                return jnp.sum(acc, axis=-1, keepdims=True)

            mu = row_sum(lambda t: t) / width
            mu_ref[rs] = mu
            inv_ref[rs] = lax.rsqrt(row_sum(lambda t: (t - mu) * (t - mu)) / width + LN_EPS)

    @pl.when(j >= nj)
    def _():
        c = j - nj
        for r in range(tm // rb):
            rs = slice(r * rb, (r + 1) * rb)
            y = (z_ref[c, rs] - mu_ref[rs]) * inv_ref[rs] * g_ref[...] + b_ref[...]
            o_ref[rs] = y
            ob_ref[rs] = y.astype(ob_ref.dtype)


def _mm_res_ln(a, wb, res, alpha, g, b, *, tm, tn, parts, name):
    m, k = a.shape
    n = wb.shape[1]
    nj = n // tn
    fill = lambda j: jnp.minimum(j, nj - 1)
    emit = lambda j: jnp.maximum(j - nj, 0)
    vec = pl.BlockSpec((1, tn), lambda i, j: (0, emit(j)))
    tile = pl.BlockSpec((tm, tn), lambda i, j: (i, emit(j)))
    return pl.pallas_call(
        functools.partial(_mm_res_ln_kernel, alpha=alpha, parts=parts),
        grid=(m // tm, 2 * nj),
        in_specs=[pl.BlockSpec((tm, k), lambda i, j: (i, 0)),
                  pl.BlockSpec((k, tn), lambda i, j: (0, fill(j))),
                  pl.BlockSpec((tm, tn), lambda i, j: (i, fill(j))),
                  vec, vec],
        out_specs=[tile, tile],
        out_shape=[jax.ShapeDtypeStruct((m, n), F32), jax.ShapeDtypeStruct((m, n), BF16)],
        scratch_shapes=[pltpu.VMEM((nj, tm, tn), F32), pltpu.VMEM((tm, 1), F32), pltpu.VMEM((tm, 1), F32)],
        compiler_params=_params("parallel", "arbitrary"),
        name=name,
    )(a, wb, res, g, b)


def _ffn_up_kernel(*refs, blocks_per_seq, parts, n_cast):
    x_ref, xp_ref, xn_ref, wu_ref, wg_ref, cw_ref, cb_ref = refs[:7]
    o_ref = refs[7 + n_cast]
    xh_ref, g_ref = refs[8 + 2 * n_cast:]
    _cast_in_kernel(refs[7:7 + n_cast], refs[8 + n_cast:8 + 2 * n_cast])
    i, j = pl.program_id(0), pl.program_id(1)
    tm = x_ref.shape[0]
    halo = xp_ref.shape[0]

    @pl.when(j == 0)
    def _():
        first = (i % blocks_per_seq) == 0
        last = (i % blocks_per_seq) == blocks_per_seq - 1
        xh_ref[0:halo] = jnp.where(first, jnp.zeros_like(xp_ref), xp_ref[...])
        xh_ref[halo:halo + tm] = x_ref[...]
        xh_ref[halo + tm:] = jnp.where(last, jnp.zeros_like(xn_ref), xn_ref[...])

    rows = tm // parts
    w = cw_ref[...]

    def matmuls(s):
        lo = 0 if s == 0 else halo + s * rows
        hi = tm + 2 * halo if s == parts - 1 else halo + (s + 1) * rows
        g_ref[lo:hi] = _dot(xh_ref[lo:hi], wg_ref[...])
        return _dot(xh_ref[halo + s * rows:halo + (s + 1) * rows], wu_ref[...])

    def gate(s, u):
        pad = SUBLANES
        win = g_ref[halo + s * rows - pad:halo + (s + 1) * rows + pad]
        own = slice(pad, pad + rows)
        g_prev = pltpu.roll(win, 1, 0)[own]
        g_next = pltpu.roll(win, rows + 2 * pad - 1, 0)[own]
        gc = g_prev * w[0:1] + win[own] * w[1:2] + g_next * w[2:3] + cb_ref[...]
        o_ref[s * rows:(s + 1) * rows] = (_silu(gc) * u).astype(o_ref.dtype)

    u_prev = matmuls(0)
    for s in range(1, parts):
        u_next = matmuls(s)
        gate(s - 1, u_prev)
        u_prev = u_next
    gate(parts - 1, u_prev)


def _ffn_up(xb, w_up_b, conv_w, conv_b, weights, seq, *, tm, tn, parts):
    m, d = xb.shape
    dff = w_up_b.shape[1] // 2
    nj = dff // tn
    halo = BF16_ROWS
    rb = tm // halo
    last_halo = m // halo - 1
    flat, slabs, rides = _cast_slabs(weights, (m // tm) * nj, lambda i, j: i * nj + j)
    out = pl.pallas_call(
        functools.partial(_ffn_up_kernel, blocks_per_seq=seq // tm, parts=parts, n_cast=len(flat)),
        grid=(m // tm, nj),
        in_specs=[pl.BlockSpec((tm, d), lambda i, j: (i, 0), pipeline_mode=pl.Buffered(1)),
                  pl.BlockSpec((halo, d), lambda i, j: (jnp.maximum(i * rb - 1, 0), 0)),
                  pl.BlockSpec((halo, d), lambda i, j: (jnp.minimum((i + 1) * rb, last_halo), 0)),
                  pl.BlockSpec((d, tn), lambda i, j: (0, j)),
                  pl.BlockSpec((d, tn), lambda i, j: (0, j + nj)),
                  pl.BlockSpec((3, tn), lambda i, j: (0, j)),
                  pl.BlockSpec((1, tn), lambda i, j: (0, j))] + slabs,
        out_specs=[pl.BlockSpec((tm, tn), lambda i, j: (i, j))] + slabs,
        out_shape=[jax.ShapeDtypeStruct((m, dff), BF16)] + [jax.ShapeDtypeStruct(f.shape, BF16) for f in flat],
        scratch_shapes=[pltpu.VMEM((tm + 2 * halo, d), BF16),
                        pltpu.VMEM((tm + 2 * halo, tn), F32)],
        compiler_params=_params("parallel", "arbitrary"),
        name="ffn_up_conv_gate",
    )(xb, xb, xb, w_up_b, w_up_b, conv_w, conv_b, *flat)
    return out[0], _cast_results(weights, rides, out[1:])


def _ln_ple_kernel(z_ref, g_ref, b_ref, pb_ref, wpg_ref, wple_ref, o_ref, x_ref, xb_ref):
    j = pl.program_id(1)
    nj, _, tn = x_ref.shape

    @pl.when(j == 0)
    def _():
        per = tn // LANES

        def store(rs, k, y):
            x_ref[k // per, rs, (k % per) * LANES:(k % per + 1) * LANES] = y
            xb_ref[rs, k * LANES:(k + 1) * LANES] = y.astype(xb_ref.dtype)

        _layer_norm_blocks(lambda rs, k: z_ref[rs, k * LANES:(k + 1) * LANES], store, g_ref, b_ref,
                           z_ref.shape[0], nj * per, LN_ROW_BLOCK)

    gate = _sigmoid(_dot(xb_ref[...], wpg_ref[...]))
    o_ref[...] = x_ref[j] + gate * _dot(pb_ref[...], wple_ref[...])


def _ln_ple(z, g, b, pb, w_pg_b, w_ple_b, *, tm, tn):
    m, d = z.shape
    dp = pb.shape[1]
    vec = pl.BlockSpec((1, d), lambda i, j: (0, 0))
    return pl.pallas_call(
        _ln_ple_kernel,
        grid=(m // tm, d // tn),
        in_specs=[pl.BlockSpec((tm, d), lambda i, j: (i, 0)),
                  vec, vec,
                  pl.BlockSpec((tm, dp), lambda i, j: (i, 0)),
                  pl.BlockSpec((d, tn), lambda i, j: (0, j)),
                  pl.BlockSpec((dp, tn), lambda i, j: (0, j))],
        out_specs=pl.BlockSpec((tm, tn), lambda i, j: (i, j)),
        out_shape=jax.ShapeDtypeStruct((m, d), F32),
        scratch_shapes=[pltpu.VMEM((d // tn, tm, tn), F32), pltpu.VMEM((tm, d), BF16)],
        compiler_params=_params("parallel", "arbitrary"),
        name="ln_ple_gate",
    )(z, g, b, pb, w_pg_b, w_ple_b)


def _rope_tables(seq):
    pos = jnp.arange(seq)
    row = (pos // GRID_W).astype(F32)
    col = (pos % GRID_W).astype(F32)
    sec = HEAD_DIM // 2
    inv = ROPE_THETA ** (-jnp.arange(0, sec, 2, dtype=F32) / sec)
    ang_r = row[:, None] * inv[None, :]
    ang_c = col[:, None] * inv[None, :]
    ang = jnp.concatenate([ang_r, ang_r, ang_c, ang_c], axis=-1)
    cos, sin = jnp.cos(ang), jnp.sin(ang)
    first_half = (jnp.arange(HEAD_DIM) % sec) < sec // 2
    sa = jnp.where(first_half[None, :], -sin, 0.0)
    sb = jnp.where(first_half[None, :], 0.0, sin)
    return cos, sa, sb


def _qk_rotary_tables(rope, q_gain, k_gain):
    cos, sa, sb = rope
    quarter = HEAD_DIM // 4

    def fold(gain, scale):
        g = gain.astype(F32) * scale
        return (cos * g[None, :], sa * jnp.roll(g, HEAD_DIM - quarter)[None, :], sb * jnp.roll(g, quarter)[None, :])

    tq = fold(q_gain, HEAD_DIM ** -0.5 * LOG2_E)
    tk = fold(k_gain, 1.0)
    return tuple(jnp.stack([a, b]) for a, b in zip(tq, tk))


def _layer(x, p, w_in, q_norm, k_norm, lb_f, lb_b, hg_norm, w_pa, w_pb, w_gate, b_gate, w_o,
           ln1_g, ln1_b, w_up, conv_w, conv_b, w_down, ln2_g, ln2_b, w_pg, w_ple,
           batch, seq, alpha, tables, tris):
    m, d = x.shape
    attn_dim = N_Q_HEADS * HEAD_DIM
    kv_dim = N_KV_HEADS * HEAD_DIM
    hq = HG_HEADS * HG_DIM
    row2 = lambda v: v.reshape(1, -1).astype(F32)
    bf = lambda v: v.astype(BF16)

    xb = bf(x)
    w_in_b = bf(w_in)
    tm = min(1024, seq)

    qkv = _qkv_proj(xb, w_in_b, _qk_rotary_tables(tables, q_norm, k_norm), seq, tm=tm, tn=512, parts=2)
    c0 = attn_dim + 2 * kv_dim
    qh = _proj(xb, w_in_b, c0, hq, 0, BF16, tm=tm, tn=512, parts=2, name="hgrn_q_proj")
    z = _proj(xb, w_in_b, c0 + hq, 2 * hq, 2 * hq // 512, F32, tm=tm, tn=512, parts=2, name="hgrn_gate_proj")
    vg = _proj(xb, w_in_b, c0 + 3 * hq, 2 * hq, hq // 512, BF16, tm=tm, tn=512, parts=2, name="hgrn_vg_proj")

    v = qkv[:, attn_dim + kv_dim:]
    vt = v.reshape(batch, seq, kv_dim).transpose(0, 2, 1).reshape(batch * kv_dim, seq)
    later_weights = [w_gate, w_pa, w_pb, w_o, w_up, w_pg, w_ple]
    y_attn, (w_gate_b, w_pa_b, w_pb_b, w_o_b, w_up_b, w_pg_b, w_ple_b) = _attention(
        qkv, vt, later_weights, batch, seq, tq=min(1024, seq), tk=min(2048, seq))

    nc = min(16, seq // HG_CHUNK)
    o_fwd = _hgrn_pass(qh, z, vg, row2(lb_f), tris[0], batch, seq, reverse=False, nc=nc, hb=4)
    y_hgrn = _hgrn_pass(qh, z, vg, row2(lb_b), tris[1], batch, seq, reverse=True,
                        prev=o_fwd, gain=row2(hg_norm), nc=nc, hb=4)

    mixed = _mix(xb, y_attn, y_hgrn, w_gate_b, row2(b_gate), w_pa_b, w_pb_b, tm=min(512, seq), tn=512)
    x1, x1b = _mm_res_ln(mixed, w_o_b, x, alpha, row2(ln1_g), row2(ln1_b),
                         tm=tm, tn=512, parts=2, name="attn_out_residual_ln")

    act, (w_down_b,) = _ffn_up(x1b, w_up_b, conv_w.astype(F32), row2(conv_b), [w_down], seq,
                               tm=min(2048, seq), tn=256, parts=4)
    z2 = _mm_res(act, w_down_b, x1, alpha, tm=min(512, seq), tn=512, name="ffn_down_residual")

    return _ln_ple(z2, row2(ln2_g), row2(ln2_b), bf(p), w_pg_b, w_ple_b, tm=min(512, seq), tn=512)


def kernel(x, p, w_in, q_norm, k_norm, lb_logits, hg_norm, w_pa, w_pb, w_gate, b_gate, w_o, ln1_g, ln1_b,
           w_up, conv_w, conv_b, w_down, ln2_g, ln2_b, w_pg, w_ple):
    batch, seq, d = x.shape
    depth = w_in.shape[0]
    alpha = (2.0 * depth) ** 0.25
    tables = _rope_tables(seq)
    tris = (_chunk_triangle(False), _chunk_triangle(True))
    lb_all = jnp.cumsum(jax.nn.softmax(lb_logits.astype(F32), axis=1), axis=1)
    h = x.reshape(batch * seq, d)
    for i in range(depth):
        h = _layer(h, p[i].reshape(batch * seq, -1), w_in[i], q_norm[i], k_norm[i], lb_all[0, i], lb_all[1, i],
                   hg_norm[i], w_pa[i], w_pb[i], w_gate[i], b_gate[i], w_o[i], ln1_g[i], ln1_b[i],
                   w_up[i], conv_w[i], conv_b[i], w_down[i], ln2_g[i], ln2_b[i], w_pg[i], w_ple[i],
                   batch, seq, alpha, tables, tris)
    return h.reshape(batch, seq, d)
```

```python
import functools

import jax
import jax.numpy as jnp
from jax import lax
from jax.experimental import pallas as pl
from jax.experimental.pallas import tpu as pltpu

F32 = jnp.float32
BF16 = jnp.bfloat16

GRID_W = 64
HEAD_DIM = 128
N_Q_HEADS = 16
N_KV_HEADS = 4
Q_PER_KV = N_Q_HEADS // N_KV_HEADS
ROPE_THETA = 10000.0
HG_HEADS = 16
HG_DIM = 128
HG_CHUNK = 64
RMS_EPS = 1e-6
LN_EPS = 1e-5

LOG2_E = 1.4426950408889634
LANES = 128
SUBLANES = 8
BF16_ROWS = 16
TRI_BLOCK = 256
LN_ROW_BLOCK = 64
VMEM_LIMIT = 56 * 1024 * 1024

NT_DIMS = (((1,), (1,)), ((), ()))
TN_DIMS = (((0,), (0,)), ((), ()))


def _params(*sem):
    return pltpu.CompilerParams(dimension_semantics=sem, vmem_limit_bytes=VMEM_LIMIT)


def _dot(a, b):
    return jnp.dot(a, b, preferred_element_type=F32)


def _sigmoid(v):
    return 0.5 + 0.5 * jnp.tanh(0.5 * v)


def _silu(v):
    h = 0.5 * v
    return h + h * jnp.tanh(h)


def _cast_slabs(weights, steps, step_index):
    tile = BF16_ROWS

    def spec(w):
        ntiles = w.shape[0] // tile
        if w.shape[0] % tile or (ntiles % steps and steps % ntiles):
            return None
        per_step, hold = max(ntiles // steps, 1), max(steps // ntiles, 1)
        return pl.BlockSpec((tile * per_step, w.shape[1]), lambda *ids: (step_index(*ids) // hold, 0))

    specs = [spec(w) for w in weights]
    rides = [s is not None for s in specs]
    return [w for w, r in zip(weights, rides) if r], [s for s in specs if s is not None], rides


def _cast_in_kernel(w_refs, wb_refs):
    for w_ref, wb_ref in zip(w_refs, wb_refs):
        wb_ref[...] = w_ref[...].astype(wb_ref.dtype)


def _cast_results(weights, rides, cast_outputs):
    cast = iter(cast_outputs)
    return [next(cast) if r else w.astype(BF16) for w, r in zip(weights, rides)]


def _matmul_row_parts(x_ref, w_ref, parts, epilogue):
    rows = x_ref.shape[0] // parts
    part = lambda s: slice(s * rows, (s + 1) * rows)
    acc_prev = _dot(x_ref[part(0)], w_ref[...])
    for s in range(1, parts):
        acc = _dot(x_ref[part(s)], w_ref[...])
        epilogue(part(s - 1), acc_prev)
        acc_prev = acc
    epilogue(part(parts - 1), acc_prev)


def _qkv_kernel(x_ref, w_ref, c0_ref, c1_ref, c2_ref, o_ref, *, nqb, nkb, parts):
    j = pl.program_id(1)
    quarter = HEAD_DIM // 4

    def norm_rope(rs, acc):
        c0, c1, c2 = c0_ref[rs], c1_ref[rs], c2_ref[rs]
        for h in range(acc.shape[1] // HEAD_DIM):
            sl = slice(h * HEAD_DIM, (h + 1) * HEAD_DIM)
            blk = acc[:, sl]
            inv = lax.rsqrt(jnp.mean(blk * blk, axis=-1, keepdims=True) + RMS_EPS)
            rot = blk * c0 + pltpu.roll(blk, HEAD_DIM - quarter, 1) * c1 + pltpu.roll(blk, quarter, 1) * c2
            o_ref[rs, sl] = (rot * inv).astype(o_ref.dtype)

    def passthrough(rs, acc):
        o_ref[rs] = acc.astype(o_ref.dtype)

    pl.when(j < nqb + nkb)(lambda: _matmul_row_parts(x_ref, w_ref, parts, norm_rope))
    pl.when(j >= nqb + nkb)(lambda: _matmul_row_parts(x_ref, w_ref, parts, passthrough))


def _qkv_proj(xb, w_in_b, tables, seq, *, tm, tn, parts):
    m, d = xb.shape
    attn_dim = N_Q_HEADS * HEAD_DIM
    kv_dim = N_KV_HEADS * HEAD_DIM
    n = attn_dim + 2 * kv_dim
    nsb = seq // tm
    nqb = attn_dim // tn
    tab = pl.BlockSpec((None, tm, HEAD_DIM), lambda i, j: ((j >= nqb).astype(jnp.int32), i % nsb, 0))
    return pl.pallas_call(
        functools.partial(_qkv_kernel, nqb=nqb, nkb=kv_dim // tn, parts=parts),
        grid=(m // tm, n // tn),
        in_specs=[pl.BlockSpec((tm, d), lambda i, j: (i, 0)),
                  pl.BlockSpec((d, tn), lambda i, j: (0, j)),
                  tab, tab, tab],
        out_specs=pl.BlockSpec((tm, tn), lambda i, j: (i, j)),
        out_shape=jax.ShapeDtypeStruct((m, n), BF16),
        compiler_params=_params("parallel", "arbitrary"),
        name="qkv_proj",
    )(xb, w_in_b, *tables)


def _proj_kernel(x_ref, w_ref, o_ref, *, silu_from, parts):
    j = pl.program_id(1)

    def store(fn):
        def epilogue(rs, acc):
            o_ref[rs] = fn(acc).astype(o_ref.dtype)
        return epilogue

    pl.when(j >= silu_from)(lambda: _matmul_row_parts(x_ref, w_ref, parts, store(_silu)))
    pl.when(j < silu_from)(lambda: _matmul_row_parts(x_ref, w_ref, parts, store(lambda acc: acc)))


def _proj(xb, wb, col0, ncols, silu_from, out_dtype, *, tm, tn, parts, name):
    m, d = xb.shape
    off = col0 // tn
    return pl.pallas_call(
        functools.partial(_proj_kernel, silu_from=silu_from, parts=parts),
        grid=(m // tm, ncols // tn),
        in_specs=[pl.BlockSpec((tm, d), lambda i, j: (i, 0)),
                  pl.BlockSpec((d, tn), lambda i, j: (0, j + off))],
        out_specs=pl.BlockSpec((tm, tn), lambda i, j: (i, j)),
        out_shape=jax.ShapeDtypeStruct((m, ncols), out_dtype),
        compiler_params=_params("parallel", "arbitrary"),
        name=name,
    )(xb, wb)


def _attn_kernel(*refs, n_cast):
    q_ref, k_ref, kn_ref, vt_ref = refs[:4]
    w_refs = refs[4:4 + n_cast]
    o_ref = refs[4 + n_cast]
    wb_refs = refs[5 + n_cast:5 + 2 * n_cast]
    m_ref, l_ref, acc_ref, s0_ref = refs[5 + 2 * n_cast:]
    ki = pl.program_id(3)

    _cast_in_kernel(w_refs, wb_refs)

    def scores_t(keys, g):
        return lax.dot_general(keys, q_ref[:, g * HEAD_DIM:(g + 1) * HEAD_DIM], NT_DIMS,
                               preferred_element_type=F32)

    @pl.when(ki == 0)
    def _():
        m_ref[...] = jnp.full_like(m_ref, -1e30)
        l_ref[...] = jnp.zeros_like(l_ref)
        acc_ref[...] = jnp.zeros_like(acc_ref)
        s0_ref[...] = scores_t(k_ref[...], 0)

    k = k_ref[...]
    vt = vt_ref[...]

    st_next = s0_ref[...]
    for g in range(Q_PER_KV):
        row = slice(g, g + 1)
        st = st_next
        if g + 1 < Q_PER_KV:
            st_next = scores_t(k, g + 1)
        else:
            s0_ref[...] = scores_t(kn_ref[...], 0)
        m_old = m_ref[row, :]
        m_new = jnp.maximum(m_old, jnp.max(st, axis=0, keepdims=True))
        alpha = jnp.exp2(m_old - m_new)
        p = jnp.exp2(st - m_new)
        l_ref[row, :] = alpha * l_ref[row, :] + jnp.sum(p, axis=0, keepdims=True)
        acc_ref[g] = alpha * acc_ref[g] + _dot(vt, p.astype(BF16))
        m_ref[row, :] = m_new

    @pl.when(ki == pl.num_programs(3) - 1)
    def _():
        for g in range(Q_PER_KV):
            o = acc_ref[g] / l_ref[g:g + 1, :]
            o_ref[:, g * HEAD_DIM:(g + 1) * HEAD_DIM] = o.T.astype(o_ref.dtype)


def _attention(qkv, vt, weights, batch, seq, *, tq, tk):
    m = qkv.shape[0]
    gw = Q_PER_KV * HEAD_DIM
    nq, nk = seq // tq, seq // tk
    k_col0 = N_Q_HEADS
    step = lambda b, h, qi, ki: ((b * N_KV_HEADS + h) * nq + qi) * nk + ki
    flat, slabs, rides = _cast_slabs(weights, batch * N_KV_HEADS * nq * nk, step)
    out = pl.pallas_call(
        functools.partial(_attn_kernel, n_cast=len(flat)),
        grid=(batch, N_KV_HEADS, nq, nk),
        in_specs=[pl.BlockSpec((tq, gw), lambda b, h, qi, ki: (b * nq + qi, h)),
                  pl.BlockSpec((tk, HEAD_DIM), lambda b, h, qi, ki: (b * nk + ki, k_col0 + h)),
                  pl.BlockSpec((tk, HEAD_DIM),
                               lambda b, h, qi, ki: (b * nk + jnp.minimum(ki + 1, nk - 1), k_col0 + h)),
                  pl.BlockSpec((HEAD_DIM, tk), lambda b, h, qi, ki: (b * N_KV_HEADS + h, ki))] + slabs,
        out_specs=[pl.BlockSpec((tq, gw), lambda b, h, qi, ki: (b * nq + qi, h))] + slabs,
        out_shape=[jax.ShapeDtypeStruct((m, N_Q_HEADS * HEAD_DIM), BF16)]
        + [jax.ShapeDtypeStruct(f.shape, BF16) for f in flat],
        scratch_shapes=[pltpu.VMEM((SUBLANES, tq), F32), pltpu.VMEM((SUBLANES, tq), F32),
                        pltpu.VMEM((Q_PER_KV, HEAD_DIM, tq), F32), pltpu.VMEM((tk, tq), F32)],
        compiler_params=_params("parallel", "parallel", "parallel", "arbitrary"),
        name="gqa_attention",
    )(qkv, qkv, qkv, vt, *flat)
    return out[0], _cast_results(weights, rides, out[1:])


def _hgrn_kernel(*refs, reverse, final, nc, hb):
    if final:
        q_ref, z_ref, v_ref, lb_ref, tri_ref, prev_ref, g_ref, gain_ref, o_ref, st_ref = refs
    else:
        q_ref, z_ref, v_ref, lb_ref, tri_ref, o_ref, st_ref = refs
    C, D = HG_CHUNK, HG_DIM
    T, W = nc * C, hb * D

    @pl.when(pl.program_id(2) == 0)
    def _():
        st_ref[...] = jnp.zeros_like(st_ref)

    lb = lb_ref[...]
    half_span = 0.5 * (1.0 - lb)
    g = half_span * jnp.tanh(0.5 * z_ref[...])
    key = half_span - g
    logf = jnp.log2(0.5 * (1.0 + lb) + g)

    hi = logf.astype(BF16)
    r1 = logf - hi.astype(F32)
    mid = r1.astype(BF16)
    lo = (r1 - mid.astype(F32)).astype(BF16)
    cat = jnp.concatenate([hi, mid, lo], axis=1)
    tri = tri_ref[...]
    parts = []
    for s in range(T // TRI_BLOCK):
        bc = _dot(tri, cat[s * TRI_BLOCK:(s + 1) * TRI_BLOCK, :])
        parts.append((bc[:, :W] + bc[:, W:2 * W]) + bc[:, 2 * W:])
    b = parts[0] if len(parts) == 1 else jnp.concatenate(parts, axis=0)
    b3 = b.reshape(nc, C, W)
    ref_row = C // 2 if reverse else C // 2 - 1
    end_row = 0 if reverse else C - 1
    bref = b3[:, ref_row:ref_row + 1, :]
    btot = b3[:, end_row:end_row + 1, :]

    q3 = q_ref[...].astype(F32).reshape(nc, C, W)
    q_in = q3 * jnp.exp2(b3 - bref)
    k_in = key.reshape(nc, C, W) * jnp.exp2(bref - b3)
    q_st = (q_in * jnp.exp2(bref)).astype(BF16).reshape(T, W)
    k_up = (k_in * jnp.exp2(btot - bref)).astype(BF16).reshape(T, W)
    dec = jnp.exp2(btot)
    q_in = q_in.astype(BF16).reshape(T, W)
    k_in = k_in.astype(BF16).reshape(T, W)

    P = 2 * C
    rowi = lax.broadcasted_iota(jnp.int32, (P, P), 0)
    coli = lax.broadcasted_iota(jnp.int32, (P, P), 1)
    mask = ((rowi >= C) == (coli >= C)) & ((coli >= rowi) if reverse else (coli <= rowi))
    order = range(nc - 1, -1, -1) if reverse else range(nc)
    for h in range(hb):
        sl = slice(h * D, (h + 1) * D)
        scores = []
        for pr in range(nc // 2):
            rows = slice(pr * P, (pr + 1) * P)
            a = lax.dot_general(q_in[rows, sl], k_in[rows, sl], NT_DIMS, preferred_element_type=F32)
            scores.append(jnp.where(mask, a, 0.0).astype(BF16))
        upd_t = [lax.dot_general(v_ref[c * C:(c + 1) * C, sl], k_up[c * C:(c + 1) * C, sl], TN_DIMS,
                                 preferred_element_type=F32) for c in range(nc)]
        state_t = st_ref[h]
        entering = [None] * nc
        for c in order:
            entering[c] = state_t.astype(BF16)
            state_t = state_t * dec[c, :, sl] + upd_t[c]
        st_ref[h] = state_t
        for pr in range(nc // 2):
            rows = slice(pr * P, (pr + 1) * P)
            o_pair = _dot(scores[pr], v_ref[rows, sl])
            for half in range(2):
                c = 2 * pr + half
                rows_c = slice(c * C, (c + 1) * C)
                o = o_pair[half * C:(half + 1) * C] + lax.dot_general(
                    q_st[rows_c, sl], entering[c], NT_DIMS, preferred_element_type=F32)
                if final:
                    tot = o + prev_ref[rows_c, sl]
                    ms = jnp.mean(tot * tot, axis=-1, keepdims=True)
                    y = tot * lax.rsqrt(ms + RMS_EPS) * gain_ref[:, sl] * g_ref[rows_c, sl].astype(F32)
                    o_ref[rows_c, sl] = y.astype(o_ref.dtype)
                else:
                    o_ref[rows_c, sl] = o


def _hgrn_pass(qh, z, vg, lb, tri, batch, seq, *, reverse, prev=None, gain=None, nc, hb):
    m, hv = qh.shape
    T, W = nc * HG_CHUNK, hb * HG_DIM
    nt, nh = seq // T, hv // W
    final = prev is not None

    def tmap(b, h, n):
        return b * nt + ((nt - 1 - n) if reverse else n)

    zoff = nh if reverse else 0
    in_specs = [pl.BlockSpec((T, W), lambda b, h, n: (tmap(b, h, n), h)),
                pl.BlockSpec((T, W), lambda b, h, n: (tmap(b, h, n), h + zoff)),
                pl.BlockSpec((T, W), lambda b, h, n: (tmap(b, h, n), h)),
                pl.BlockSpec((1, W), lambda b, h, n: (0, h)),
                pl.BlockSpec((TRI_BLOCK, TRI_BLOCK), lambda b, h, n: (0, 0))]
    args = [qh, z, vg, lb, tri]
    if final:
        in_specs += [pl.BlockSpec((T, W), lambda b, h, n: (tmap(b, h, n), h)),
                     pl.BlockSpec((T, W), lambda b, h, n: (tmap(b, h, n), h + nh)),
                     pl.BlockSpec((1, W), lambda b, h, n: (0, h))]
        args += [prev, vg, gain]
    return pl.pallas_call(
        functools.partial(_hgrn_kernel, reverse=reverse, final=final, nc=nc, hb=hb),
        grid=(batch, nh, nt),
        in_specs=in_specs,
        out_specs=pl.BlockSpec((T, W), lambda b, h, n: (tmap(b, h, n), h)),
        out_shape=jax.ShapeDtypeStruct((m, hv), BF16 if final else F32),
        scratch_shapes=[pltpu.VMEM((hb, HG_DIM, HG_DIM), F32)],
        compiler_params=_params("parallel", "parallel", "arbitrary"),
        name="hgrn2_bwd_merge" if final else "hgrn2_fwd",
    )(*args)


def _chunk_triangle(reverse):
    i = jnp.arange(TRI_BLOCK)
    same = (i[:, None] // HG_CHUNK) == (i[None, :] // HG_CHUNK)
    tri = (i[None, :] >= i[:, None]) if reverse else (i[None, :] <= i[:, None])
    return (same & tri).astype(BF16)


def _mix_kernel(x_ref, ya_ref, yh_ref, wga_ref, wgh_ref, wpa_ref, wpb_ref, bga_ref, bgh_ref, o_ref):
    x = x_ref[...]
    ga = _sigmoid(_dot(x, wga_ref[...]) + bga_ref[...])
    gh = _sigmoid(_dot(x, wgh_ref[...]) + bgh_ref[...])
    o = ga * _dot(ya_ref[...], wpa_ref[...]) + gh * _dot(yh_ref[...], wpb_ref[...])
    o_ref[...] = o.astype(o_ref.dtype)


def _mix(xb, ya, yh, w_gate_b, b_gate, w_pa_b, w_pb_b, *, tm, tn):
    m, d = xb.shape
    da, dh = ya.shape[1], yh.shape[1]
    nj = d // tn
    return pl.pallas_call(
        _mix_kernel,
        grid=(m // tm, nj),
        in_specs=[pl.BlockSpec((tm, d), lambda i, j: (i, 0)),
                  pl.BlockSpec((tm, da), lambda i, j: (i, 0)),
                  pl.BlockSpec((tm, dh), lambda i, j: (i, 0)),
                  pl.BlockSpec((d, tn), lambda i, j: (0, j)),
                  pl.BlockSpec((d, tn), lambda i, j: (0, j + nj)),
                  pl.BlockSpec((da, tn), lambda i, j: (0, j)),
                  pl.BlockSpec((dh, tn), lambda i, j: (0, j)),
                  pl.BlockSpec((1, tn), lambda i, j: (0, j)),
                  pl.BlockSpec((1, tn), lambda i, j: (0, j + nj))],
        out_specs=pl.BlockSpec((tm, tn), lambda i, j: (i, j)),
        out_shape=jax.ShapeDtypeStruct((m, d), BF16),
        compiler_params=_params("parallel", "arbitrary"),
        name="gated_branch_merge",
    )(xb, ya, yh, w_gate_b, w_gate_b, w_pa_b, w_pb_b, b_gate, b_gate)


def _mm_res_kernel(a_ref, w_ref, res_ref, o_ref, *, alpha):
    o_ref[...] = alpha * res_ref[...] + _dot(a_ref[...], w_ref[...])


def _mm_res(a, wb, res3, alpha, *, tm, name):
    m, k = a.shape
    nj, _, tn = res3.shape
    return pl.pallas_call(
        functools.partial(_mm_res_kernel, alpha=alpha),
        grid=(m // tm, nj),
        in_specs=[pl.BlockSpec((tm, k), lambda i, j: (i, 0)),
                  pl.BlockSpec((k, tn), lambda i, j: (0, j)),
                  pl.BlockSpec((None, tm, tn), lambda i, j: (j, i, 0))],
        out_specs=pl.BlockSpec((tm, tn), lambda i, j: (i, j)),
        out_shape=jax.ShapeDtypeStruct((m, nj * tn), F32),
        compiler_params=_params("parallel", "arbitrary"),
        name=name,
    )(a, wb, res3)


def _layer_norm_blocks(load, store, g_ref, b_ref, rows, chunks, cw, rb):
    width = chunks * cw
    for r in range(rows // rb):
        rs = slice(r * rb, (r + 1) * rb)

        def row_sum(fn):
            acc = fn(load(rs, 0))
            for c in range(1, chunks):
                acc = acc + fn(load(rs, c))
            return jnp.sum(acc, axis=-1, keepdims=True)

        mu = row_sum(lambda t: t) / width
        inv = lax.rsqrt(row_sum(lambda t: (t - mu) * (t - mu)) / width + LN_EPS)
        for c in range(chunks):
            cols = slice(c * cw, (c + 1) * cw)
            store(rs, c, (load(rs, c) - mu) * inv * g_ref[:, cols] + b_ref[:, cols])


def _mm_res_ln_kernel(a_ref, w_ref, res_ref, g_ref, b_ref, o_ref, ob_ref, *, alpha):
    j = pl.program_id(1)
    nj, _, tn = o_ref.shape
    o_ref[j] = alpha * res_ref[...] + _dot(a_ref[...], w_ref[...])

    @pl.when(j == nj - 1)
    def _():
        def store(rs, c, y):
            o_ref[c, rs] = y
            ob_ref[rs, c * tn:(c + 1) * tn] = y.astype(ob_ref.dtype)

        _layer_norm_blocks(lambda rs, c: o_ref[c, rs], store, g_ref, b_ref, o_ref.shape[1], nj, tn, LN_ROW_BLOCK)


def _mm_res_ln(a, wb, res, alpha, g, b, *, tm, tn, name):
    m, k = a.shape
    n = wb.shape[1]
    nj = n // tn
    vec = pl.BlockSpec((1, n), lambda i, j: (0, 0))
    return pl.pallas_call(
        functools.partial(_mm_res_ln_kernel, alpha=alpha),
        grid=(m // tm, nj),
        in_specs=[pl.BlockSpec((tm, k), lambda i, j: (i, 0)),
                  pl.BlockSpec((k, tn), lambda i, j: (0, j)),
                  pl.BlockSpec((tm, tn), lambda i, j: (i, j)),
                  vec, vec],
        out_specs=[pl.BlockSpec((nj, tm, tn), lambda i, j: (0, i, 0)),
                   pl.BlockSpec((tm, n), lambda i, j: (i, 0))],
        out_shape=[jax.ShapeDtypeStruct((nj, m, tn), F32), jax.ShapeDtypeStruct((m, n), BF16)],
        compiler_params=_params("parallel", "arbitrary"),
        name=name,
    )(a, wb, res, g, b)


def _ffn_up_kernel(*refs, blocks_per_seq, parts, n_cast):
    x_ref, xp_ref, xn_ref, wu_ref, wg_ref, cw_ref, cb_ref = refs[:7]
    o_ref = refs[7 + n_cast]
    xh_ref, g_ref = refs[8 + 2 * n_cast:]
    _cast_in_kernel(refs[7:7 + n_cast], refs[8 + n_cast:8 + 2 * n_cast])
    i, j = pl.program_id(0), pl.program_id(1)
    tm = x_ref.shape[0]
    halo = xp_ref.shape[0]

    @pl.when(j == 0)
    def _():
        first = (i % blocks_per_seq) == 0
        last = (i % blocks_per_seq) == blocks_per_seq - 1
        xh_ref[0:halo] = jnp.where(first, jnp.zeros_like(xp_ref), xp_ref[...])
        xh_ref[halo:halo + tm] = x_ref[...]
        xh_ref[halo + tm:] = jnp.where(last, jnp.zeros_like(xn_ref), xn_ref[...])

    rows = tm // parts
    w = cw_ref[...]

    def matmuls(s):
        lo = 0 if s == 0 else halo + s * rows
        hi = tm + 2 * halo if s == parts - 1 else halo + (s + 1) * rows
        g_ref[lo:hi] = _dot(xh_ref[lo:hi], wg_ref[...])
        return _dot(xh_ref[halo + s * rows:halo + (s + 1) * rows], wu_ref[...])

    def gate(s, u):
        pad = SUBLANES
        win = g_ref[halo + s * rows - pad:halo + (s + 1) * rows + pad]
        own = slice(pad, pad + rows)
        g_prev = pltpu.roll(win, 1, 0)[own]
        g_next = pltpu.roll(win, rows + 2 * pad - 1, 0)[own]
        gc = g_prev * w[0:1] + win[own] * w[1:2] + g_next * w[2:3] + cb_ref[...]
        o_ref[s * rows:(s + 1) * rows] = (_silu(gc) * u).astype(o_ref.dtype)

    u_prev = matmuls(0)
    for s in range(1, parts):
        u_next = matmuls(s)
        gate(s - 1, u_prev)
        u_prev = u_next
    gate(parts - 1, u_prev)


def _ffn_up(xb, w_up_b, conv_w, conv_b, weights, seq, *, tm, tn, parts):
    m, d = xb.shape
    dff = w_up_b.shape[1] // 2
    nj = dff // tn
    halo = BF16_ROWS
    rb = tm // halo
    last_halo = m // halo - 1
    flat, slabs, rides = _cast_slabs(weights, (m // tm) * nj, lambda i, j: i * nj + j)
    out = pl.pallas_call(
        functools.partial(_ffn_up_kernel, blocks_per_seq=seq // tm, parts=parts, n_cast=len(flat)),
        grid=(m // tm, nj),
        in_specs=[pl.BlockSpec((tm, d), lambda i, j: (i, 0), pipeline_mode=pl.Buffered(1)),
                  pl.BlockSpec((halo, d), lambda i, j: (jnp.maximum(i * rb - 1, 0), 0)),
                  pl.BlockSpec((halo, d), lambda i, j: (jnp.minimum((i + 1) * rb, last_halo), 0)),
                  pl.BlockSpec((d, tn), lambda i, j: (0, j)),
                  pl.BlockSpec((d, tn), lambda i, j: (0, j + nj)),
                  pl.BlockSpec((3, tn), lambda i, j: (0, j)),
                  pl.BlockSpec((1, tn), lambda i, j: (0, j))] + slabs,
        out_specs=[pl.BlockSpec((tm, tn), lambda i, j: (i, j))] + slabs,
        out_shape=[jax.ShapeDtypeStruct((m, dff), BF16)] + [jax.ShapeDtypeStruct(f.shape, BF16) for f in flat],
        scratch_shapes=[pltpu.VMEM((tm + 2 * halo, d), BF16),
                        pltpu.VMEM((tm + 2 * halo, tn), F32)],
        compiler_params=_params("parallel", "arbitrary"),
        name="ffn_up_conv_gate",
    )(xb, xb, xb, w_up_b, w_up_b, conv_w, conv_b, *flat)
    return out[0], _cast_results(weights, rides, out[1:])


def _ln_ple_kernel(z_ref, g_ref, b_ref, pb_ref, wpg_ref, wple_ref, o_ref, x_ref, xb_ref):
    j = pl.program_id(1)
    nj, _, tn = x_ref.shape

    @pl.when(j == 0)
    def _():
        def store(rs, c, y):
            x_ref[c, rs] = y
            xb_ref[rs, c * tn:(c + 1) * tn] = y.astype(xb_ref.dtype)

        _layer_norm_blocks(lambda rs, c: z_ref[rs, c * tn:(c + 1) * tn], store, g_ref, b_ref,
                           z_ref.shape[0], nj, tn, LN_ROW_BLOCK)

    gate = _sigmoid(_dot(xb_ref[...], wpg_ref[...]))
    o_ref[...] = x_ref[j] + gate * _dot(pb_ref[...], wple_ref[...])


def _ln_ple(z, g, b, pb, w_pg_b, w_ple_b, *, tm, tn):
    m, d = z.shape
    dp = pb.shape[1]
    vec = pl.BlockSpec((1, d), lambda i, j: (0, 0))
    return pl.pallas_call(
        _ln_ple_kernel,
        grid=(m // tm, d // tn),
        in_specs=[pl.BlockSpec((tm, d), lambda i, j: (i, 0)),
                  vec, vec,
                  pl.BlockSpec((tm, dp), lambda i, j: (i, 0)),
                  pl.BlockSpec((d, tn), lambda i, j: (0, j)),
                  pl.BlockSpec((dp, tn), lambda i, j: (0, j))],
        out_specs=pl.BlockSpec((tm, tn), lambda i, j: (i, j)),
        out_shape=jax.ShapeDtypeStruct((m, d), F32),
        scratch_shapes=[pltpu.VMEM((d // tn, tm, tn), F32), pltpu.VMEM((tm, d), BF16)],
        compiler_params=_params("parallel", "arbitrary"),
        name="ln_ple_gate",
    )(z, g, b, pb, w_pg_b, w_ple_b)


def _rope_tables(seq):
    pos = jnp.arange(seq)
    row = (pos // GRID_W).astype(F32)
    col = (pos % GRID_W).astype(F32)
    sec = HEAD_DIM // 2
    inv = ROPE_THETA ** (-jnp.arange(0, sec, 2, dtype=F32) / sec)
    ang_r = row[:, None] * inv[None, :]
    ang_c = col[:, None] * inv[None, :]
    ang = jnp.concatenate([ang_r, ang_r, ang_c, ang_c], axis=-1)
    cos, sin = jnp.cos(ang), jnp.sin(ang)
    first_half = (jnp.arange(HEAD_DIM) % sec) < sec // 2
    sa = jnp.where(first_half[None, :], -sin, 0.0)
    sb = jnp.where(first_half[None, :], 0.0, sin)
    return cos, sa, sb


def _qk_rotary_tables(rope, q_gain, k_gain):
    cos, sa, sb = rope
    quarter = HEAD_DIM // 4

    def fold(gain, scale):
        g = gain.astype(F32) * scale
        return (cos * g[None, :], sa * jnp.roll(g, HEAD_DIM - quarter)[None, :], sb * jnp.roll(g, quarter)[None, :])

    tq = fold(q_gain, HEAD_DIM ** -0.5 * LOG2_E)
    tk = fold(k_gain, 1.0)
    return tuple(jnp.stack([a, b]) for a, b in zip(tq, tk))


def _layer(x, p, w_in, q_norm, k_norm, lb_f, lb_b, hg_norm, w_pa, w_pb, w_gate, b_gate, w_o,
           ln1_g, ln1_b, w_up, conv_w, conv_b, w_down, ln2_g, ln2_b, w_pg, w_ple,
           batch, seq, alpha, tables, tris):
    m, d = x.shape
    attn_dim = N_Q_HEADS * HEAD_DIM
    kv_dim = N_KV_HEADS * HEAD_DIM
    hq = HG_HEADS * HG_DIM
    row2 = lambda v: v.reshape(1, -1).astype(F32)
    bf = lambda v: v.astype(BF16)

    xb = bf(x)
    w_in_b = bf(w_in)
    tm = min(1024, seq)

    qkv = _qkv_proj(xb, w_in_b, _qk_rotary_tables(tables, q_norm, k_norm), seq, tm=tm, tn=512, parts=2)
    c0 = attn_dim + 2 * kv_dim
    qh = _proj(xb, w_in_b, c0, hq, 0, BF16, tm=tm, tn=512, parts=2, name="hgrn_q_proj")
    z = _proj(xb, w_in_b, c0 + hq, 2 * hq, 2 * hq // 512, F32, tm=tm, tn=512, parts=2, name="hgrn_gate_proj")
    vg = _proj(xb, w_in_b, c0 + 3 * hq, 2 * hq, hq // 512, BF16, tm=tm, tn=512, parts=2, name="hgrn_vg_proj")

    v = qkv[:, attn_dim + kv_dim:]
    vt = v.reshape(batch, seq, kv_dim).transpose(0, 2, 1).reshape(batch * kv_dim, seq)
    later_weights = [w_gate, w_pa, w_pb, w_o, w_up, w_pg, w_ple]
    y_attn, (w_gate_b, w_pa_b, w_pb_b, w_o_b, w_up_b, w_pg_b, w_ple_b) = _attention(
        qkv, vt, later_weights, batch, seq, tq=min(1024, seq), tk=min(2048, seq))

    nc = min(16, seq // HG_CHUNK)
    o_fwd = _hgrn_pass(qh, z, vg, row2(lb_f), tris[0], batch, seq, reverse=False, nc=nc, hb=4)
    y_hgrn = _hgrn_pass(qh, z, vg, row2(lb_b), tris[1], batch, seq, reverse=True,
                        prev=o_fwd, gain=row2(hg_norm), nc=nc, hb=4)

    mixed = _mix(xb, y_attn, y_hgrn, w_gate_b, row2(b_gate), w_pa_b, w_pb_b, tm=min(512, seq), tn=512)
    x1_blocked, x1b = _mm_res_ln(mixed, w_o_b, x, alpha, row2(ln1_g), row2(ln1_b),
                                 tm=min(512, seq), tn=512, name="attn_out_residual_ln")

    act, (w_down_b,) = _ffn_up(x1b, w_up_b, conv_w.astype(F32), row2(conv_b), [w_down], seq,
                               tm=min(2048, seq), tn=256, parts=4)
    z2 = _mm_res(act, w_down_b, x1_blocked, alpha, tm=min(512, seq), name="ffn_down_residual")

    return _ln_ple(z2, row2(ln2_g), row2(ln2_b), bf(p), w_pg_b, w_ple_b, tm=min(512, seq), tn=512)


def kernel(x, p, w_in, q_norm, k_norm, lb_logits, hg_norm, w_pa, w_pb, w_gate, b_gate, w_o, ln1_g, ln1_b,
           w_up, conv_w, conv_b, w_down, ln2_g, ln2_b, w_pg, w_ple):
    batch, seq, d = x.shape
    depth = w_in.shape[0]
    alpha = (2.0 * depth) ** 0.25
    tables = _rope_tables(seq)
    tris = (_chunk_triangle(False), _chunk_triangle(True))
    lb_all = jnp.cumsum(jax.nn.softmax(lb_logits.astype(F32), axis=1), axis=1)
    h = x.reshape(batch * seq, d)
    for i in range(depth):
        h = _layer(h, p[i].reshape(batch * seq, -1), w_in[i], q_norm[i], k_norm[i], lb_all[0, i], lb_all[1, i],
                   hg_norm[i], w_pa[i], w_pb[i], w_gate[i], b_gate[i], w_o[i], ln1_g[i], ln1_b[i],
                   w_up[i], conv_w[i], conv_b[i], w_down[i], ln2_g[i], ln2_b[i], w_pg[i], w_ple[i],
                   batch, seq, alpha, tables, tris)
    return h.reshape(batch, seq, d)
```

```python
import functools

import jax
import jax.numpy as jnp
from jax import lax
from jax.experimental import pallas as pl
from jax.experimental.pallas import tpu as pltpu

F32 = jnp.float32
BF16 = jnp.bfloat16

GRID_W = 64
HEAD_DIM = 128
N_Q_HEADS = 16
N_KV_HEADS = 4
Q_PER_KV = N_Q_HEADS // N_KV_HEADS
ROPE_THETA = 10000.0
HG_HEADS = 16
HG_DIM = 128
HG_CHUNK = 64
RMS_EPS = 1e-6
LN_EPS = 1e-5

LOG2_E = 1.4426950408889634
LANES = 128
SUBLANES = 8
BF16_ROWS = 16
TRI_BLOCK = 256
LN_ROW_BLOCK = 64
VMEM_LIMIT = 56 * 1024 * 1024

NT_DIMS = (((1,), (1,)), ((), ()))
TN_DIMS = (((0,), (0,)), ((), ()))


def _params(*sem):
    return pltpu.CompilerParams(dimension_semantics=sem, vmem_limit_bytes=VMEM_LIMIT)


def _dot(a, b):
    return jnp.dot(a, b, preferred_element_type=F32)


def _sigmoid(v):
    return 0.5 + 0.5 * jnp.tanh(0.5 * v)


def _silu(v):
    h = 0.5 * v
    return h + h * jnp.tanh(h)


def _cast_slabs(weights, steps, step_index):
    tile = BF16_ROWS

    def spec(w):
        ntiles = w.shape[0] // tile
        if w.shape[0] % tile or (ntiles % steps and steps % ntiles):
            return None
        per_step, hold = max(ntiles // steps, 1), max(steps // ntiles, 1)
        return pl.BlockSpec((tile * per_step, w.shape[1]), lambda *ids: (step_index(*ids) // hold, 0))

    specs = [spec(w) for w in weights]
    rides = [s is not None for s in specs]
    return [w for w, r in zip(weights, rides) if r], [s for s in specs if s is not None], rides


def _cast_in_kernel(w_refs, wb_refs):
    for w_ref, wb_ref in zip(w_refs, wb_refs):
        wb_ref[...] = w_ref[...].astype(wb_ref.dtype)


def _cast_results(weights, rides, cast_outputs):
    cast = iter(cast_outputs)
    return [next(cast) if r else w.astype(BF16) for w, r in zip(weights, rides)]


def _matmul_row_parts(x_ref, w_ref, parts, epilogue):
    rows = x_ref.shape[0] // parts
    part = lambda s: slice(s * rows, (s + 1) * rows)
    acc_prev = _dot(x_ref[part(0)], w_ref[...])
    for s in range(1, parts):
        acc = _dot(x_ref[part(s)], w_ref[...])
        epilogue(part(s - 1), acc_prev)
        acc_prev = acc
    epilogue(part(parts - 1), acc_prev)


def _qkv_kernel(x_ref, w_ref, c0_ref, c1_ref, c2_ref, o_ref, *, nqb, nkb, parts):
    j = pl.program_id(1)
    quarter = HEAD_DIM // 4

    def norm_rope(rs, acc):
        c0, c1, c2 = c0_ref[rs], c1_ref[rs], c2_ref[rs]
        for h in range(acc.shape[1] // HEAD_DIM):
            sl = slice(h * HEAD_DIM, (h + 1) * HEAD_DIM)
            blk = acc[:, sl]
            inv = lax.rsqrt(jnp.mean(blk * blk, axis=-1, keepdims=True) + RMS_EPS)
            rot = blk * c0 + pltpu.roll(blk, HEAD_DIM - quarter, 1) * c1 + pltpu.roll(blk, quarter, 1) * c2
            o_ref[rs, sl] = (rot * inv).astype(o_ref.dtype)

    def passthrough(rs, acc):
        o_ref[rs] = acc.astype(o_ref.dtype)

    pl.when(j < nqb + nkb)(lambda: _matmul_row_parts(x_ref, w_ref, parts, norm_rope))
    pl.when(j >= nqb + nkb)(lambda: _matmul_row_parts(x_ref, w_ref, parts, passthrough))


def _qkv_proj(xb, w_in_b, tables, seq, *, tm, tn, parts):
    m, d = xb.shape
    attn_dim = N_Q_HEADS * HEAD_DIM
    kv_dim = N_KV_HEADS * HEAD_DIM
    n = attn_dim + 2 * kv_dim
    nsb = seq // tm
    nqb = attn_dim // tn
    tab = pl.BlockSpec((None, tm, HEAD_DIM), lambda i, j: ((j >= nqb).astype(jnp.int32), i % nsb, 0))
    return pl.pallas_call(
        functools.partial(_qkv_kernel, nqb=nqb, nkb=kv_dim // tn, parts=parts),
        grid=(m // tm, n // tn),
        in_specs=[pl.BlockSpec((tm, d), lambda i, j: (i, 0)),
                  pl.BlockSpec((d, tn), lambda i, j: (0, j)),
                  tab, tab, tab],
        out_specs=pl.BlockSpec((tm, tn), lambda i, j: (i, j)),
        out_shape=jax.ShapeDtypeStruct((m, n), BF16),
        compiler_params=_params("parallel", "arbitrary"),
        name="qkv_proj",
    )(xb, w_in_b, *tables)


def _proj_kernel(x_ref, w_ref, o_ref, *, silu, parts):
    j = pl.program_id(1)
    with_silu = silu(j)

    def store(fn):
        def epilogue(rs, acc):
            o_ref[rs] = fn(acc).astype(o_ref.dtype)
        return epilogue

    pl.when(with_silu)(lambda: _matmul_row_parts(x_ref, w_ref, parts, store(_silu)))
    pl.when(jnp.logical_not(with_silu))(lambda: _matmul_row_parts(x_ref, w_ref, parts, store(lambda acc: acc)))


def _proj(xb, wb, source_block, ncols, silu, out_dtype, *, tm, tn, parts, name):
    m, d = xb.shape
    return pl.pallas_call(
        functools.partial(_proj_kernel, silu=silu, parts=parts),
        grid=(m // tm, ncols // tn),
        in_specs=[pl.BlockSpec((tm, d), lambda i, j: (i, 0)),
                  pl.BlockSpec((d, tn), lambda i, j: (0, source_block(j)))],
        out_specs=pl.BlockSpec((tm, tn), lambda i, j: (i, j)),
        out_shape=jax.ShapeDtypeStruct((m, ncols), out_dtype),
        compiler_params=_params("parallel", "arbitrary"),
        name=name,
    )(xb, wb)


def _attn_kernel(*refs, n_cast):
    q_ref, k_ref, kn_ref, vt_ref = refs[:4]
    w_refs = refs[4:4 + n_cast]
    o_ref = refs[4 + n_cast]
    wb_refs = refs[5 + n_cast:5 + 2 * n_cast]
    m_ref, l_ref, acc_ref, s0_ref = refs[5 + 2 * n_cast:]
    ki = pl.program_id(3)

    _cast_in_kernel(w_refs, wb_refs)

    def scores_t(keys, g):
        return lax.dot_general(keys, q_ref[:, g * HEAD_DIM:(g + 1) * HEAD_DIM], NT_DIMS,
                               preferred_element_type=F32)

    @pl.when(ki == 0)
    def _():
        m_ref[...] = jnp.full_like(m_ref, -1e30)
        l_ref[...] = jnp.zeros_like(l_ref)
        acc_ref[...] = jnp.zeros_like(acc_ref)
        s0_ref[...] = scores_t(k_ref[...], 0)

    k = k_ref[...]
    vt = vt_ref[...]

    st_next = s0_ref[...]
    for g in range(Q_PER_KV):
        row = slice(g, g + 1)
        st = st_next
        if g + 1 < Q_PER_KV:
            st_next = scores_t(k, g + 1)
        else:
            s0_ref[...] = scores_t(kn_ref[...], 0)
        m_old = m_ref[row, :]
        m_new = jnp.maximum(m_old, jnp.max(st, axis=0, keepdims=True))
        alpha = jnp.exp2(m_old - m_new)
        p = jnp.exp2(st - m_new)
        l_ref[row, :] = alpha * l_ref[row, :] + jnp.sum(p, axis=0, keepdims=True)
        acc_ref[g] = alpha * acc_ref[g] + _dot(vt, p.astype(BF16))
        m_ref[row, :] = m_new

    @pl.when(ki == pl.num_programs(3) - 1)
    def _():
        for g in range(Q_PER_KV):
            o = acc_ref[g] / l_ref[g:g + 1, :]
            o_ref[:, g * HEAD_DIM:(g + 1) * HEAD_DIM] = o.T.astype(o_ref.dtype)


def _attention(qkv, vt, weights, batch, seq, *, tq, tk):
    m = qkv.shape[0]
    gw = Q_PER_KV * HEAD_DIM
    nq, nk = seq // tq, seq // tk
    k_col0 = N_Q_HEADS
    step = lambda b, h, qi, ki: ((b * N_KV_HEADS + h) * nq + qi) * nk + ki
    flat, slabs, rides = _cast_slabs(weights, batch * N_KV_HEADS * nq * nk, step)
    out = pl.pallas_call(
        functools.partial(_attn_kernel, n_cast=len(flat)),
        grid=(batch, N_KV_HEADS, nq, nk),
        in_specs=[pl.BlockSpec((tq, gw), lambda b, h, qi, ki: (b * nq + qi, h)),
                  pl.BlockSpec((tk, HEAD_DIM), lambda b, h, qi, ki: (b * nk + ki, k_col0 + h)),
                  pl.BlockSpec((tk, HEAD_DIM),
                               lambda b, h, qi, ki: (b * nk + jnp.minimum(ki + 1, nk - 1), k_col0 + h)),
                  pl.BlockSpec((HEAD_DIM, tk), lambda b, h, qi, ki: (b * N_KV_HEADS + h, ki))] + slabs,
        out_specs=[pl.BlockSpec((tq, gw), lambda b, h, qi, ki: (b * nq + qi, h))] + slabs,
        out_shape=[jax.ShapeDtypeStruct((m, N_Q_HEADS * HEAD_DIM), BF16)]
        + [jax.ShapeDtypeStruct(f.shape, BF16) for f in flat],
        scratch_shapes=[pltpu.VMEM((SUBLANES, tq), F32), pltpu.VMEM((SUBLANES, tq), F32),
                        pltpu.VMEM((Q_PER_KV, HEAD_DIM, tq), F32), pltpu.VMEM((tk, tq), F32)],
        compiler_params=_params("parallel", "parallel", "parallel", "arbitrary"),
        name="gqa_attention",
    )(qkv, qkv, qkv, vt, *flat)
    return out[0], _cast_results(weights, rides, out[1:])


def _hgrn_kernel(*refs, reverse, final, nc, hb):
    if final:
        q_ref, z_ref, v_ref, lb_ref, tri_ref, prev_ref, g_ref, gain_ref, o_ref, st_ref = refs
    else:
        q_ref, z_ref, v_ref, lb_ref, tri_ref, o_ref, st_ref = refs
    C, D = HG_CHUNK, HG_DIM
    T, W = nc * C, hb * D

    @pl.when(pl.program_id(2) == 0)
    def _():
        st_ref[...] = jnp.zeros_like(st_ref)

    lb = lb_ref[...]
    half_span = 0.5 * (1.0 - lb)
    g = half_span * jnp.tanh(0.5 * z_ref[...])
    key = half_span - g
    logf = jnp.log2(0.5 * (1.0 + lb) + g)

    hi = logf.astype(BF16)
    r1 = logf - hi.astype(F32)
    mid = r1.astype(BF16)
    lo = (r1 - mid.astype(F32)).astype(BF16)
    cat = jnp.concatenate([hi, mid, lo], axis=1)
    tri = tri_ref[...]
    parts = []
    for s in range(T // TRI_BLOCK):
        bc = _dot(tri, cat[s * TRI_BLOCK:(s + 1) * TRI_BLOCK, :])
        parts.append((bc[:, :W] + bc[:, W:2 * W]) + bc[:, 2 * W:])
    b = parts[0] if len(parts) == 1 else jnp.concatenate(parts, axis=0)
    b3 = b.reshape(nc, C, W)
    ref_row = C // 2 if reverse else C // 2 - 1
    end_row = 0 if reverse else C - 1
    bref = b3[:, ref_row:ref_row + 1, :]
    btot = b3[:, end_row:end_row + 1, :]

    q3 = q_ref[...].astype(F32).reshape(nc, C, W)
    q_in = q3 * jnp.exp2(b3 - bref)
    k_in = key.reshape(nc, C, W) * jnp.exp2(bref - b3)
    q_st = (q_in * jnp.exp2(bref)).astype(BF16).reshape(T, W)
    k_up = (k_in * jnp.exp2(btot - bref)).astype(BF16).reshape(T, W)
    dec = jnp.exp2(btot)
    q_in = q_in.astype(BF16).reshape(T, W)
    k_in = k_in.astype(BF16).reshape(T, W)

    P = 2 * C
    rowi = lax.broadcasted_iota(jnp.int32, (P, P), 0)
    coli = lax.broadcasted_iota(jnp.int32, (P, P), 1)
    mask = ((rowi >= C) == (coli >= C)) & ((coli >= rowi) if reverse else (coli <= rowi))
    order = range(nc - 1, -1, -1) if reverse else range(nc)
    for h in range(hb):
        sl = slice(h * D, (h + 1) * D)
        scores = []
        for pr in range(nc // 2):
            rows = slice(pr * P, (pr + 1) * P)
            a = lax.dot_general(q_in[rows, sl], k_in[rows, sl], NT_DIMS, preferred_element_type=F32)
            scores.append(jnp.where(mask, a, 0.0).astype(BF16))
        upd_t = [lax.dot_general(v_ref[c * C:(c + 1) * C, sl], k_up[c * C:(c + 1) * C, sl], TN_DIMS,
                                 preferred_element_type=F32) for c in range(nc)]
        state_t = st_ref[h]
        entering = [None] * nc
        for c in order:
            entering[c] = state_t.astype(BF16)
            state_t = state_t * dec[c, :, sl] + upd_t[c]
        st_ref[h] = state_t
        for pr in range(nc // 2):
            rows = slice(pr * P, (pr + 1) * P)
            o_pair = _dot(scores[pr], v_ref[rows, sl])
            for half in range(2):
                c = 2 * pr + half
                rows_c = slice(c * C, (c + 1) * C)
                o = o_pair[half * C:(half + 1) * C] + lax.dot_general(
                    q_st[rows_c, sl], entering[c], NT_DIMS, preferred_element_type=F32)
                if final:
                    tot = o + prev_ref[rows_c, sl]
                    ms = jnp.mean(tot * tot, axis=-1, keepdims=True)
                    y = tot * lax.rsqrt(ms + RMS_EPS) * gain_ref[:, sl] * g_ref[rows_c, sl].astype(F32)
                    o_ref[rows_c, sl] = y.astype(o_ref.dtype)
                else:
                    o_ref[rows_c, sl] = o


def _hgrn_pass(qvg, z, lb, tri, batch, seq, *, reverse, prev=None, gain=None, nc, hb):
    m, hv = qvg.shape[0], qvg.shape[1] // 3
    T, W = nc * HG_CHUNK, hb * HG_DIM
    nt, nh = seq // T, hv // W
    final = prev is not None

    def tmap(b, h, n):
        return b * nt + ((nt - 1 - n) if reverse else n)

    zoff = nh if reverse else 0
    in_specs = [pl.BlockSpec((T, W), lambda b, h, n: (tmap(b, h, n), h)),
                pl.BlockSpec((T, W), lambda b, h, n: (tmap(b, h, n), h + zoff)),
                pl.BlockSpec((T, W), lambda b, h, n: (tmap(b, h, n), h + nh)),
                pl.BlockSpec((1, W), lambda b, h, n: (0, h)),
                pl.BlockSpec((TRI_BLOCK, TRI_BLOCK), lambda b, h, n: (0, 0))]
    args = [qvg, z, qvg, lb, tri]
    if final:
        in_specs += [pl.BlockSpec((T, W), lambda b, h, n: (tmap(b, h, n), h)),
                     pl.BlockSpec((T, W), lambda b, h, n: (tmap(b, h, n), h + 2 * nh)),
                     pl.BlockSpec((1, W), lambda b, h, n: (0, h))]
        args += [prev, qvg, gain]
    return pl.pallas_call(
        functools.partial(_hgrn_kernel, reverse=reverse, final=final, nc=nc, hb=hb),
        grid=(batch, nh, nt),
        in_specs=in_specs,
        out_specs=pl.BlockSpec((T, W), lambda b, h, n: (tmap(b, h, n), h)),
        out_shape=jax.ShapeDtypeStruct((m, hv), BF16 if final else F32),
        scratch_shapes=[pltpu.VMEM((hb, HG_DIM, HG_DIM), F32)],
        compiler_params=_params("parallel", "parallel", "arbitrary"),
        name="hgrn2_bwd_merge" if final else "hgrn2_fwd",
    )(*args)


def _chunk_triangle(reverse):
    i = jnp.arange(TRI_BLOCK)
    same = (i[:, None] // HG_CHUNK) == (i[None, :] // HG_CHUNK)
    tri = (i[None, :] >= i[:, None]) if reverse else (i[None, :] <= i[:, None])
    return (same & tri).astype(BF16)


def _mix_kernel(x_ref, ya_ref, yh_ref, wga_ref, wgh_ref, wpa_ref, wpb_ref, bga_ref, bgh_ref, o_ref):
    x = x_ref[...]
    ga = _sigmoid(_dot(x, wga_ref[...]) + bga_ref[...])
    gh = _sigmoid(_dot(x, wgh_ref[...]) + bgh_ref[...])
    o = ga * _dot(ya_ref[...], wpa_ref[...]) + gh * _dot(yh_ref[...], wpb_ref[...])
    o_ref[...] = o.astype(o_ref.dtype)


def _mix(xb, ya, yh, w_gate_b, b_gate, w_pa_b, w_pb_b, *, tm, tn):
    m, d = xb.shape
    da, dh = ya.shape[1], yh.shape[1]
    nj = d // tn
    return pl.pallas_call(
        _mix_kernel,
        grid=(m // tm, nj),
        in_specs=[pl.BlockSpec((tm, d), lambda i, j: (i, 0)),
                  pl.BlockSpec((tm, da), lambda i, j: (i, 0)),
                  pl.BlockSpec((tm, dh), lambda i, j: (i, 0)),
                  pl.BlockSpec((d, tn), lambda i, j: (0, j)),
                  pl.BlockSpec((d, tn), lambda i, j: (0, j + nj)),
                  pl.BlockSpec((da, tn), lambda i, j: (0, j)),
                  pl.BlockSpec((dh, tn), lambda i, j: (0, j)),
                  pl.BlockSpec((1, tn), lambda i, j: (0, j)),
                  pl.BlockSpec((1, tn), lambda i, j: (0, j + nj))],
        out_specs=pl.BlockSpec((tm, tn), lambda i, j: (i, j)),
        out_shape=jax.ShapeDtypeStruct((m, d), BF16),
        compiler_params=_params("parallel", "arbitrary"),
        name="gated_branch_merge",
    )(xb, ya, yh, w_gate_b, w_gate_b, w_pa_b, w_pb_b, b_gate, b_gate)


def _mm_res_kernel(a_ref, w_ref, res_ref, o_ref, *, alpha):
    o_ref[...] = alpha * res_ref[...] + _dot(a_ref[...], w_ref[...])


def _mm_res(a, wb, res3, alpha, *, tm, name):
    m, k = a.shape
    nj, _, tn = res3.shape
    return pl.pallas_call(
        functools.partial(_mm_res_kernel, alpha=alpha),
        grid=(m // tm, nj),
        in_specs=[pl.BlockSpec((tm, k), lambda i, j: (i, 0)),
                  pl.BlockSpec((k, tn), lambda i, j: (0, j)),
                  pl.BlockSpec((None, tm, tn), lambda i, j: (j, i, 0))],
        out_specs=pl.BlockSpec((tm, tn), lambda i, j: (i, j)),
        out_shape=jax.ShapeDtypeStruct((m, nj * tn), F32),
        compiler_params=_params("parallel", "arbitrary"),
        name=name,
    )(a, wb, res3)


def _layer_norm_blocks(load, store, g_ref, b_ref, rows, chunks, cw, rb):
    width = chunks * cw
    for r in range(rows // rb):
        rs = slice(r * rb, (r + 1) * rb)

        def row_sum(fn):
            acc = fn(load(rs, 0))
            for c in range(1, chunks):
                acc = acc + fn(load(rs, c))
            return jnp.sum(acc, axis=-1, keepdims=True)

        mu = row_sum(lambda t: t) / width
        inv = lax.rsqrt(row_sum(lambda t: (t - mu) * (t - mu)) / width + LN_EPS)
        for c in range(chunks):
            cols = slice(c * cw, (c + 1) * cw)
            store(rs, c, (load(rs, c) - mu) * inv * g_ref[:, cols] + b_ref[:, cols])


def _mm_res_ln_kernel(a_ref, w_ref, res_ref, g_ref, b_ref, o_ref, ob_ref, *, alpha):
    j = pl.program_id(1)
    nj, _, tn = o_ref.shape
    o_ref[j] = alpha * res_ref[...] + _dot(a_ref[...], w_ref[...])

    @pl.when(j == nj - 1)
    def _():
        def store(rs, c, y):
            o_ref[c, rs] = y
            ob_ref[rs, c * tn:(c + 1) * tn] = y.astype(ob_ref.dtype)

        _layer_norm_blocks(lambda rs, c: o_ref[c, rs], store, g_ref, b_ref, o_ref.shape[1], nj, tn, LN_ROW_BLOCK)


def _mm_res_ln(a, wb, res, alpha, g, b, *, tm, tn, name):
    m, k = a.shape
    n = wb.shape[1]
    nj = n // tn
    vec = pl.BlockSpec((1, n), lambda i, j: (0, 0))
    return pl.pallas_call(
        functools.partial(_mm_res_ln_kernel, alpha=alpha),
        grid=(m // tm, nj),
        in_specs=[pl.BlockSpec((tm, k), lambda i, j: (i, 0)),
                  pl.BlockSpec((k, tn), lambda i, j: (0, j)),
                  pl.BlockSpec((tm, tn), lambda i, j: (i, j)),
                  vec, vec],
        out_specs=[pl.BlockSpec((nj, tm, tn), lambda i, j: (0, i, 0)),
                   pl.BlockSpec((tm, n), lambda i, j: (i, 0))],
        out_shape=[jax.ShapeDtypeStruct((nj, m, tn), F32), jax.ShapeDtypeStruct((m, n), BF16)],
        compiler_params=_params("parallel", "arbitrary"),
        name=name,
    )(a, wb, res, g, b)


def _ffn_up_kernel(*refs, blocks_per_seq, parts, n_cast):
    x_ref, xp_ref, xn_ref, wu_ref, wg_ref, cw_ref, cb_ref = refs[:7]
    o_ref = refs[7 + n_cast]
    xh_ref, g_ref = refs[8 + 2 * n_cast:]
    _cast_in_kernel(refs[7:7 + n_cast], refs[8 + n_cast:8 + 2 * n_cast])
    i, j = pl.program_id(0), pl.program_id(1)
    tm = x_ref.shape[0]
    halo = xp_ref.shape[0]

    @pl.when(j == 0)
    def _():
        first = (i % blocks_per_seq) == 0
        last = (i % blocks_per_seq) == blocks_per_seq - 1
        xh_ref[0:halo] = jnp.where(first, jnp.zeros_like(xp_ref), xp_ref[...])
        xh_ref[halo:halo + tm] = x_ref[...]
        xh_ref[halo + tm:] = jnp.where(last, jnp.zeros_like(xn_ref), xn_ref[...])

    rows = tm // parts
    w = cw_ref[...]

    def matmuls(s):
        lo = 0 if s == 0 else halo + s * rows
        hi = tm + 2 * halo if s == parts - 1 else halo + (s + 1) * rows
        g_ref[lo:hi] = _dot(xh_ref[lo:hi], wg_ref[...])
        return _dot(xh_ref[halo + s * rows:halo + (s + 1) * rows], wu_ref[...])

    def gate(s, u):
        pad = SUBLANES
        win = g_ref[halo + s * rows - pad:halo + (s + 1) * rows + pad]
        own = slice(pad, pad + rows)
        g_prev = pltpu.roll(win, 1, 0)[own]
        g_next = pltpu.roll(win, rows + 2 * pad - 1, 0)[own]
        gc = g_prev * w[0:1] + win[own] * w[1:2] + g_next * w[2:3] + cb_ref[...]
        o_ref[s * rows:(s + 1) * rows] = (_silu(gc) * u).astype(o_ref.dtype)

    u_prev = matmuls(0)
    for s in range(1, parts):
        u_next = matmuls(s)
        gate(s - 1, u_prev)
        u_prev = u_next
    gate(parts - 1, u_prev)


def _ffn_up(xb, w_up_b, conv_w, conv_b, weights, seq, *, tm, tn, parts):
    m, d = xb.shape
    dff = w_up_b.shape[1] // 2
    nj = dff // tn
    halo = BF16_ROWS
    rb = tm // halo
    last_halo = m // halo - 1
    flat, slabs, rides = _cast_slabs(weights, (m // tm) * nj, lambda i, j: i * nj + j)
    out = pl.pallas_call(
        functools.partial(_ffn_up_kernel, blocks_per_seq=seq // tm, parts=parts, n_cast=len(flat)),
        grid=(m // tm, nj),
        in_specs=[pl.BlockSpec((tm, d), lambda i, j: (i, 0), pipeline_mode=pl.Buffered(1)),
                  pl.BlockSpec((halo, d), lambda i, j: (jnp.maximum(i * rb - 1, 0), 0)),
                  pl.BlockSpec((halo, d), lambda i, j: (jnp.minimum((i + 1) * rb, last_halo), 0)),
                  pl.BlockSpec((d, tn), lambda i, j: (0, j)),
                  pl.BlockSpec((d, tn), lambda i, j: (0, j + nj)),
                  pl.BlockSpec((3, tn), lambda i, j: (0, j)),
                  pl.BlockSpec((1, tn), lambda i, j: (0, j))] + slabs,
        out_specs=[pl.BlockSpec((tm, tn), lambda i, j: (i, j))] + slabs,
        out_shape=[jax.ShapeDtypeStruct((m, dff), BF16)] + [jax.ShapeDtypeStruct(f.shape, BF16) for f in flat],
        scratch_shapes=[pltpu.VMEM((tm + 2 * halo, d), BF16),
                        pltpu.VMEM((tm + 2 * halo, tn), F32)],
        compiler_params=_params("parallel", "arbitrary"),
        name="ffn_up_conv_gate",
    )(xb, xb, xb, w_up_b, w_up_b, conv_w, conv_b, *flat)
    return out[0], _cast_results(weights, rides, out[1:])


def _ln_ple_kernel(z_ref, g_ref, b_ref, pb_ref, wpg_ref, wple_ref, o_ref, x_ref, xb_ref):
    j = pl.program_id(1)
    nj, _, tn = x_ref.shape

    @pl.when(j == 0)
    def _():
        def store(rs, c, y):
            x_ref[c, rs] = y
            xb_ref[rs, c * tn:(c + 1) * tn] = y.astype(xb_ref.dtype)

        _layer_norm_blocks(lambda rs, c: z_ref[rs, c * tn:(c + 1) * tn], store, g_ref, b_ref,
                           z_ref.shape[0], nj, tn, LN_ROW_BLOCK)

    gate = _sigmoid(_dot(xb_ref[...], wpg_ref[...]))
    o_ref[...] = x_ref[j] + gate * _dot(pb_ref[...], wple_ref[...])


def _ln_ple(z, g, b, pb, w_pg_b, w_ple_b, *, tm, tn):
    m, d = z.shape
    dp = pb.shape[1]
    vec = pl.BlockSpec((1, d), lambda i, j: (0, 0))
    return pl.pallas_call(
        _ln_ple_kernel,
        grid=(m // tm, d // tn),
        in_specs=[pl.BlockSpec((tm, d), lambda i, j: (i, 0)),
                  vec, vec,
                  pl.BlockSpec((tm, dp), lambda i, j: (i, 0)),
                  pl.BlockSpec((d, tn), lambda i, j: (0, j)),
                  pl.BlockSpec((dp, tn), lambda i, j: (0, j))],
        out_specs=pl.BlockSpec((tm, tn), lambda i, j: (i, j)),
        out_shape=jax.ShapeDtypeStruct((m, d), F32),
        scratch_shapes=[pltpu.VMEM((d // tn, tm, tn), F32), pltpu.VMEM((tm, d), BF16)],
        compiler_params=_params("parallel", "arbitrary"),
        name="ln_ple_gate",
    )(z, g, b, pb, w_pg_b, w_ple_b)


def _rope_tables(seq):
    pos = jnp.arange(seq)
    row = (pos // GRID_W).astype(F32)
    col = (pos % GRID_W).astype(F32)
    sec = HEAD_DIM // 2
    inv = ROPE_THETA ** (-jnp.arange(0, sec, 2, dtype=F32) / sec)
    ang_r = row[:, None] * inv[None, :]
    ang_c = col[:, None] * inv[None, :]
    ang = jnp.concatenate([ang_r, ang_r, ang_c, ang_c], axis=-1)
    cos, sin = jnp.cos(ang), jnp.sin(ang)
    first_half = (jnp.arange(HEAD_DIM) % sec) < sec // 2
    sa = jnp.where(first_half[None, :], -sin, 0.0)
    sb = jnp.where(first_half[None, :], 0.0, sin)
    return cos, sa, sb


def _qk_rotary_tables(rope, q_gain, k_gain):
    cos, sa, sb = rope
    quarter = HEAD_DIM // 4

    def fold(gain, scale):
        g = gain.astype(F32) * scale
        return (cos * g[None, :], sa * jnp.roll(g, HEAD_DIM - quarter)[None, :], sb * jnp.roll(g, quarter)[None, :])

    tq = fold(q_gain, HEAD_DIM ** -0.5 * LOG2_E)
    tk = fold(k_gain, 1.0)
    return tuple(jnp.stack([a, b]) for a, b in zip(tq, tk))


def _layer(x, p, w_in, q_norm, k_norm, lb_f, lb_b, hg_norm, w_pa, w_pb, w_gate, b_gate, w_o,
           ln1_g, ln1_b, w_up, conv_w, conv_b, w_down, ln2_g, ln2_b, w_pg, w_ple,
           batch, seq, alpha, tables, tris):
    m, d = x.shape
    attn_dim = N_Q_HEADS * HEAD_DIM
    kv_dim = N_KV_HEADS * HEAD_DIM
    hq = HG_HEADS * HG_DIM
    row2 = lambda v: v.reshape(1, -1).astype(F32)
    bf = lambda v: v.astype(BF16)

    xb = bf(x)
    w_in_b = bf(w_in)
    tm = min(1024, seq)

    qkv = _qkv_proj(xb, w_in_b, _qk_rotary_tables(tables, q_norm, k_norm), seq, tm=tm, tn=512, parts=2)
    c0 = attn_dim + 2 * kv_dim
    cb, hb_ = c0 // 512, hq // 512
    qvg = _proj(xb, w_in_b, lambda j: cb + jnp.where(j < hb_, j, j + 2 * hb_), 3 * hq,
                lambda j: (j < hb_) | (j >= 2 * hb_), BF16, tm=tm, tn=512, parts=2, name="hgrn_qvg_proj")
    z = _proj(xb, w_in_b, lambda j: cb + hb_ + j, 2 * hq, lambda j: j < 0, F32,
              tm=tm, tn=512, parts=2, name="hgrn_gate_proj")

    v = qkv[:, attn_dim + kv_dim:]
    vt = v.reshape(batch, seq, kv_dim).transpose(0, 2, 1).reshape(batch * kv_dim, seq)
    later_weights = [w_gate, w_pa, w_pb, w_o, w_up, w_pg, w_ple]
    y_attn, (w_gate_b, w_pa_b, w_pb_b, w_o_b, w_up_b, w_pg_b, w_ple_b) = _attention(
        qkv, vt, later_weights, batch, seq, tq=min(1024, seq), tk=min(2048, seq))

    nc = min(16, seq // HG_CHUNK)
    o_fwd = _hgrn_pass(qvg, z, row2(lb_f), tris[0], batch, seq, reverse=False, nc=nc, hb=4)
    y_hgrn = _hgrn_pass(qvg, z, row2(lb_b), tris[1], batch, seq, reverse=True,
                        prev=o_fwd, gain=row2(hg_norm), nc=nc, hb=4)

    mixed = _mix(xb, y_attn, y_hgrn, w_gate_b, row2(b_gate), w_pa_b, w_pb_b, tm=min(512, seq), tn=512)
    x1_blocked, x1b = _mm_res_ln(mixed, w_o_b, x, alpha, row2(ln1_g), row2(ln1_b),
                                 tm=min(512, seq), tn=512, name="attn_out_residual_ln")

    act, (w_down_b,) = _ffn_up(x1b, w_up_b, conv_w.astype(F32), row2(conv_b), [w_down], seq,
                               tm=min(2048, seq), tn=256, parts=4)
    z2 = _mm_res(act, w_down_b, x1_blocked, alpha, tm=min(512, seq), name="ffn_down_residual")

    return _ln_ple(z2, row2(ln2_g), row2(ln2_b), bf(p), w_pg_b, w_ple_b, tm=min(512, seq), tn=512)


def kernel(x, p, w_in, q_norm, k_norm, lb_logits, hg_norm, w_pa, w_pb, w_gate, b_gate, w_o, ln1_g, ln1_b,
           w_up, conv_w, conv_b, w_down, ln2_g, ln2_b, w_pg, w_ple):
    batch, seq, d = x.shape
    depth = w_in.shape[0]
    alpha = (2.0 * depth) ** 0.25
    tables = _rope_tables(seq)
    tris = (_chunk_triangle(False), _chunk_triangle(True))
    lb_all = jnp.cumsum(jax.nn.softmax(lb_logits.astype(F32), axis=1), axis=1)
    h = x.reshape(batch * seq, d)
    for i in range(depth):
        h = _layer(h, p[i].reshape(batch * seq, -1), w_in[i], q_norm[i], k_norm[i], lb_all[0, i], lb_all[1, i],
                   hg_norm[i], w_pa[i], w_pb[i], w_gate[i], b_gate[i], w_o[i], ln1_g[i], ln1_b[i],
                   w_up[i], conv_w[i], conv_b[i], w_down[i], ln2_g[i], ln2_b[i], w_pg[i], w_ple[i],
                   batch, seq, alpha, tables, tris)
    return h.reshape(batch, seq, d)
```
